```python
import jax, jax.numpy as jnp
from jax import lax
import numpy as np

D_MODEL = 1024
BATCH = 16
SEQ = 2048
DEPTH = 4
DEC_BATCH = 8
DEC_SEQ = 4096
PAST_LEN = 128

GRID_W = 64
RMS_EPS = 1e-6
F_FLOOR = 1e-30
MASK_VALUE = -1e30
D_FF = 2816
HG_HEADS = 4
HG_DK = 128
HG_DV = 128
HG_KWIDTH = HG_HEADS * HG_DK
HG_WIDTH = HG_HEADS * HG_DV
HG_CHUNK = 64
CV_GROUPS = 4
CV_GROUP_DIM = 128
CV_WIDTH = CV_GROUPS * CV_GROUP_DIM
CONV_K = 3
NA_HEADS = 8
NA_DH = 64
NA_WIDTH = NA_HEADS * NA_DH
NA_KH = 8
NA_KW = 16
NA_QB = 16
NA_NQB = GRID_W // NA_QB
NA_KBW = NA_QB + NA_KW
NA_SCALE = NA_DH ** -0.5
IN_WIDTHS = (HG_KWIDTH, HG_WIDTH, HG_KWIDTH, HG_KWIDTH, HG_WIDTH,
             CV_WIDTH, CV_WIDTH, CV_WIDTH,
             NA_WIDTH, NA_WIDTH, NA_WIDTH,
             D_MODEL, D_MODEL, D_MODEL)
N_IN = sum(IN_WIDTHS)

kernel_name = 'hybrid_bidir_hgrn2_conv_natten_encoder'


def rmsnorm(x, gain):
    x32 = x.astype(jnp.float32)
    y = x32 * lax.rsqrt(jnp.mean(x32 * x32, axis=-1, keepdims=True) + RMS_EPS)
    return (y * gain.astype(jnp.float32)).astype(x.dtype)


def swiglu_ffn(x, norm_g, w_gu, w_down):
    h = rmsnorm(x, norm_g) @ w_gu
    a, b = jnp.split(h, 2, axis=-1)
    return (jax.nn.silu(a) * b) @ w_down


def hgrn2_lower_bounds(lb_logits):
    p = jax.nn.softmax(lb_logits.astype(jnp.float32), axis=1)
    return jnp.cumsum(p, axis=1) - p[:, :1]


def hgrn2_gates(z, lb):
    z32 = z.astype(jnp.float32)
    f = lb + (1.0 - lb) * jax.nn.sigmoid(z32)
    logf = jnp.log(jnp.maximum(f, F_FLOOR))
    k = (1.0 - lb) * jax.nn.sigmoid(-z32)
    return k, logf


def hgrn2_chunk_scan(q, k, v, logf):
    B, T, H, DK = q.shape
    DV = v.shape[-1]
    C = HG_CHUNK
    N = T // C

    def to_chunks(a):
        return a.reshape(B, N, C, H, a.shape[-1]).transpose(1, 0, 3, 2, 4)

    tri = jnp.asarray(np.tril(np.ones((C, C), dtype=bool)))[:, :, None]

    def step(S, inp):
        qc, kc, vc, lc = inp
        b = jnp.cumsum(lc, axis=2)
        o_inter = jnp.einsum('bhtk,bhkv->bhtv', qc * jnp.exp(b), S)
        diff = b[:, :, :, None, :] - b[:, :, None, :, :]
        decay = jnp.where(tri, jnp.exp(jnp.where(tri, diff, 0.0)), 0.0)
        A = jnp.einsum('bhtk,bhtsk->bhts', qc, decay * kc[:, :, None, :, :])
        o_intra = jnp.einsum('bhts,bhsv->bhtv', A, vc)
        b_end = b[:, :, -1, :]
        S = jnp.exp(b_end)[..., None] * S + jnp.einsum(
            'bhsk,bhsv->bhkv', kc * jnp.exp(b_end[:, :, None, :] - b), vc)
        return S, o_inter + o_intra

    S0 = jnp.zeros((B, H, DK, DV), jnp.float32)
    _, o = lax.scan(step, S0, (to_chunks(q), to_chunks(k), to_chunks(v), to_chunks(logf)))
    return o.transpose(1, 0, 3, 2, 4).reshape(B, T, H, DV)


def neighbourhood_attention(q, k, v, rpb):
    B, T, H, dh = q.shape
    rows = T // GRID_W
    kh = min(NA_KH, rows)
    qg = q.reshape(B, rows, NA_NQB, NA_QB, H, dh).transpose(1, 0, 2, 3, 4, 5)
    kg = k.reshape(B, rows, GRID_W, H, dh)
    vg = v.reshape(B, rows, GRID_W, H, dh)
    qcol = np.arange(GRID_W).reshape(NA_NQB, NA_QB)
    kstart = np.clip(np.arange(NA_NQB) * NA_QB - NA_KW // 2, 0, GRID_W - NA_KBW)
    kcol = kstart[:, None] + np.arange(NA_KBW)
    cs = np.clip(qcol - NA_KW // 2, 0, GRID_W - NA_KW)
    col_ok = jnp.asarray((kcol[:, None, :] >= cs[..., None]) & (kcol[:, None, :] < cs[..., None] + NA_KW))
    dc = np.clip(kcol[:, None, :] - qcol[..., None] + NA_KW - 1, 0, 2 * NA_KW - 2)
    rpb_col = rpb.astype(jnp.float32)[:, :, dc]

    def one_row(args):
        r, q_row = args
        rs = jnp.clip(r - kh // 2, 0, rows - kh)
        k_win = lax.dynamic_slice_in_dim(kg, rs, kh, axis=1)[:, :, kcol]
        v_win = lax.dynamic_slice_in_dim(vg, rs, kh, axis=1)[:, :, kcol]
        s = jnp.einsum('bjqhd,bajkhd->bhjqak', q_row, k_win).astype(jnp.float32) * NA_SCALE
        dr = rs + jnp.arange(kh) - r + NA_KH - 1
        bias = jnp.take(rpb_col, dr, axis=1).transpose(0, 2, 3, 1, 4)
        s = jnp.where(col_ok[:, :, None, :], s + bias, MASK_VALUE)
        p = jax.nn.softmax(s, axis=(-2, -1))
        o = jnp.einsum('bhjqak,bajkhd->bjqhd', p.astype(v_win.dtype), v_win)
        return o.reshape(B, GRID_W, H * dh)

    out = lax.map(one_row, (jnp.arange(rows), qg))
    return out.transpose(1, 0, 2, 3).reshape(B, T, H * dh)


def hybrid_mixer(u, layer_lb, w_in, hg_out_norm, w_hg_out, conv_w, conv_b, w_cv_out,
                 na_rpb, w_na_out, w_out):
    B, T, _ = u.shape
    proj = u @ w_in
    splits = np.cumsum(IN_WIDTHS)[:-1].tolist()
    (hq, hi, hzf, hzb, hg, ca, cb, cc, nq, nk, nv, g_hg, g_cv, g_na) = jnp.split(proj, splits, axis=-1)

    q = hq.reshape(B, T, HG_HEADS, HG_DK).astype(jnp.float32)
    v = hi.reshape(B, T, HG_HEADS, HG_DV).astype(jnp.float32)
    k_f, logf_f = hgrn2_gates(hzf.reshape(B, T, HG_HEADS, HG_DK), layer_lb[0].reshape(HG_HEADS, HG_DK))
    k_b, logf_b = hgrn2_gates(hzb.reshape(B, T, HG_HEADS, HG_DK), layer_lb[1].reshape(HG_HEADS, HG_DK))
    flip = lambda a: jnp.flip(a, axis=1)
    o = hgrn2_chunk_scan(q, k_f, v, logf_f) + flip(hgrn2_chunk_scan(flip(q), flip(k_b), flip(v), flip(logf_b)))
    o = o * lax.rsqrt(jnp.mean(o * o, axis=-1, keepdims=True) + RMS_EPS) \
        * hg_out_norm.astype(jnp.float32).reshape(HG_HEADS, HG_DV)
    o = o.reshape(B, T, HG_WIDTH) * jax.nn.silu(hg.astype(jnp.float32))
    y_hg = o.astype(u.dtype) @ w_hg_out

    z = cc * ca
    zc = lax.conv_general_dilated(z, conv_w[:, None, :].astype(z.dtype), window_strides=(1,),
                                  padding=((CONV_K // 2, CONV_K // 2),),
                                  dimension_numbers=('NWC', 'WIO', 'NWC'),
                                  feature_group_count=CV_WIDTH)
    y_cv = (cb * (zc + conv_b)) @ w_cv_out

    o_na = neighbourhood_attention(nq.reshape(B, T, NA_HEADS, NA_DH), nk.reshape(B, T, NA_HEADS, NA_DH),
                                   nv.reshape(B, T, NA_HEADS, NA_DH), na_rpb)
    y_na = o_na @ w_na_out

    m = jax.nn.sigmoid(g_hg) * y_hg + jax.nn.sigmoid(g_cv) * y_cv + jax.nn.sigmoid(g_na) * y_na
    return m @ w_out


def run_trunk(x, ffn1_norm, ffn1_w_gu, ffn1_w_down, mix_norm, w_in, hg_lb_logits, hg_out_norm,
              w_hg_out, conv_w, conv_b, w_cv_out, na_rpb, w_na_out, w_out,
              ffn2_norm, ffn2_w_gu, ffn2_w_down, final_norm):
    lb_all = hgrn2_lower_bounds(hg_lb_logits)
    for l in range(DEPTH):
        x = x + 0.5 * swiglu_ffn(x, ffn1_norm[l], ffn1_w_gu[l], ffn1_w_down[l])
        x = x + hybrid_mixer(rmsnorm(x, mix_norm[l]), lb_all[:, l], w_in[l], hg_out_norm[l], w_hg_out[l],
                             conv_w[l], conv_b[l], w_cv_out[l], na_rpb[l], w_na_out[l], w_out[l])
        x = x + 0.5 * swiglu_ffn(x, ffn2_norm[l], ffn2_w_gu[l], ffn2_w_down[l])
    return rmsnorm(x, final_norm)


def setup_inputs(seed: int = 0) -> dict:
    key = jax.random.key(seed)
    ks = jax.random.split(key, 24)
    nrm = lambda k, shape, scale: jax.random.normal(k, shape, jnp.float32) * scale
    L, D = DEPTH, D_MODEL
    return {
        'x_prompt': nrm(ks[0], (BATCH, SEQ, D), 1.0),
        'x_sample': nrm(ks[1], (DEC_BATCH, DEC_SEQ, D), 1.0),
        'ffn1_norm': 1.0 + nrm(ks[2], (L, D), 0.02),
        'ffn1_w_gu': nrm(ks[3], (L, D, 2 * D_FF), D ** -0.5),
        'ffn1_w_down': nrm(ks[4], (L, D_FF, D), D_FF ** -0.5),
        'mix_norm': 1.0 + nrm(ks[5], (L, D), 0.02),
        'w_in': nrm(ks[6], (L, D, N_IN), D ** -0.5),
        'hg_lb_logits': 1.0 + nrm(ks[7], (2, L, HG_KWIDTH), 0.1),
        'hg_out_norm': 1.0 + nrm(ks[8], (L, HG_WIDTH), 0.02),
        'w_hg_out': nrm(ks[9], (L, HG_WIDTH, D), HG_WIDTH ** -0.5),
        'conv_w': nrm(ks[10], (L, CONV_K, CV_WIDTH), CONV_K ** -0.5),
        'conv_b': nrm(ks[11], (L, CV_WIDTH), 0.01),
        'w_cv_out': nrm(ks[12], (L, CV_WIDTH, D), CV_WIDTH ** -0.5),
        'na_rpb': nrm(ks[13], (L, NA_HEADS, 2 * NA_KH - 1, 2 * NA_KW - 1), 0.1),
        'w_na_out': nrm(ks[14], (L, NA_WIDTH, D), NA_WIDTH ** -0.5),
        'w_out': nrm(ks[15], (L, D, D), D ** -0.5),
        'ffn2_norm': 1.0 + nrm(ks[16], (L, D), 0.02),
        'ffn2_w_gu': nrm(ks[17], (L, D, 2 * D_FF), D ** -0.5),
        'ffn2_w_down': nrm(ks[18], (L, D_FF, D), D_FF ** -0.5),
        'final_norm': 1.0 + nrm(ks[19], (D,), 0.02),
    }


def reference(x_prompt, x_sample, ffn1_norm, ffn1_w_gu, ffn1_w_down, mix_norm, w_in, hg_lb_logits,
              hg_out_norm, w_hg_out, conv_w, conv_b, w_cv_out, na_rpb, w_na_out, w_out,
              ffn2_norm, ffn2_w_gu, ffn2_w_down, final_norm):
    y_prompt = run_trunk(x_prompt, ffn1_norm, ffn1_w_gu, ffn1_w_down, mix_norm, w_in, hg_lb_logits,
                         hg_out_norm, w_hg_out, conv_w, conv_b, w_cv_out, na_rpb, w_na_out, w_out,
                         ffn2_norm, ffn2_w_gu, ffn2_w_down, final_norm)
    y_sample = run_trunk(x_sample, ffn1_norm, ffn1_w_gu, ffn1_w_down, mix_norm, w_in, hg_lb_logits,
                         hg_out_norm, w_hg_out, conv_w, conv_b, w_cv_out, na_rpb, w_na_out, w_out,
                         ffn2_norm, ffn2_w_gu, ffn2_w_down, final_norm)
    return (y_prompt, y_sample)
```

```python
import functools

import numpy as np
import jax
import jax.numpy as jnp
from jax import lax
from jax.experimental import pallas as pl
from jax.experimental.pallas import tpu as pltpu

F32 = jnp.float32
BF16 = jnp.bfloat16

GRID_W = 64
RMS_EPS = 1e-6
F_FLOOR = 1e-30
MASK_VALUE = -1e30
HG_HEADS, HG_DK = 4, 128
CV_GROUP = 128
NA_HEADS, NA_DH = 8, 64
NA_KH, NA_KW = 8, 16
NA_SCALE = NA_DH ** -0.5
MIX_W = HG_HEADS * HG_DK

LANES = 128
VMEM_LIMIT = 56 * 1024 * 1024

TM_FFN = 512
HG_C = 64
HG_SB = 16
HG_NSB = HG_C // HG_SB

_COL = dict(g_hg=0, g_cv=8, g_na=16, hq=24, hi=28, hzf=32, hzb=36, hg=40,
            ca=44, cb=48, cc=52, nq=56, nk=60, nv=64)
N_IN = 68 * LANES

_NT = (((1,), (1,)), ((), ()))
_TN = (((0,), (0,)), ((), ()))


def _cparams(sem):
    return pltpu.CompilerParams(dimension_semantics=sem, vmem_limit_bytes=VMEM_LIMIT)


def _rms_scale(x, gain):
    return x * lax.rsqrt(jnp.mean(x * x, axis=-1, keepdims=True) + RMS_EPS) * gain


def _ffn_body(x_ref, g_ref, wg_ref, wu_ref, wd_ref, o_ref, xn_ref, acc_ref):
    j = pl.program_id(1)

    @pl.when(j == 0)
    def _():
        xn_ref[...] = _rms_scale(x_ref[...], g_ref[...]).astype(BF16)
        acc_ref[...] = jnp.zeros_like(acc_ref)

    xn = xn_ref[...]
    a = jnp.dot(xn, wg_ref[...], preferred_element_type=F32)
    b = jnp.dot(xn, wu_ref[...], preferred_element_type=F32)
    h = (a * jax.nn.sigmoid(a) * b).astype(BF16)
    acc_ref[...] += jnp.dot(h, wd_ref[...], preferred_element_type=F32)

    @pl.when(j == pl.num_programs(1) - 1)
    def _():
        o_ref[...] = x_ref[...] + 0.5 * acc_ref[...]


def _ffn(x, gain, w_gu, w_down, layer):
    n, d = x.shape
    d_ff = w_down.shape[1]
    tf = d_ff // 2 if (d_ff // 2) % LANES == 0 else d_ff
    nj = d_ff // tf
    return pl.pallas_call(
        _ffn_body,
        grid=(n // TM_FFN, nj),
        in_specs=[
            pl.BlockSpec((TM_FFN, d), lambda i, j: (i, 0)),
            pl.BlockSpec((None, 1, d), lambda i, j: (layer, 0, 0)),
            pl.BlockSpec((None, d, tf), lambda i, j: (layer, 0, j)),
            pl.BlockSpec((None, d, tf), lambda i, j: (layer, 0, j + nj)),
            pl.BlockSpec((None, tf, d), lambda i, j: (layer, j, 0)),
        ],
        out_specs=pl.BlockSpec((TM_FFN, d), lambda i, j: (i, 0)),
        out_shape=jax.ShapeDtypeStruct((n, d), F32),
        scratch_shapes=[pltpu.VMEM((TM_FFN, d), BF16), pltpu.VMEM((TM_FFN, d), F32)],
        compiler_params=_cparams(("parallel", "arbitrary")),
    )(x, gain, w_gu, w_gu, w_down)


def _inproj_body(x_ref, g_ref, w_ref, o_ref, xn_ref):
    @pl.when(pl.program_id(1) == 0)
    def _():
        xn_ref[...] = _rms_scale(x_ref[...], g_ref[...]).astype(BF16)

    o_ref[...] = jnp.dot(xn_ref[...], w_ref[...], preferred_element_type=F32).astype(BF16)


def _inproj(x, gain, w_in, layer):
    n, d = x.shape
    n_in = w_in.shape[-1]
    tn = n_in // 4
    return pl.pallas_call(
        _inproj_body,
        grid=(n // TM_FFN, n_in // tn),
        in_specs=[
            pl.BlockSpec((TM_FFN, d), lambda i, j: (i, 0)),
            pl.BlockSpec((None, 1, d), lambda i, j: (layer, 0, 0)),
            pl.BlockSpec((None, d, tn), lambda i, j: (layer, 0, j)),
        ],
        out_specs=pl.BlockSpec((TM_FFN, tn), lambda i, j: (i, j)),
        out_shape=jax.ShapeDtypeStruct((n, n_in), BF16),
        scratch_shapes=[pltpu.VMEM((TM_FFN, d), BF16)],
        compiler_params=_cparams(("parallel", "arbitrary")),
    )(x, gain, w_in)


def _hg_gates(z, lb):
    f = lb + (1.0 - lb) * jax.nn.sigmoid(z)
    logf = jnp.log(jnp.maximum(f, F_FLOOR))
    k = (1.0 - lb) * jax.nn.sigmoid(-z)
    return k, logf


def _cumsum_rows(tri, x):
    hi = x.astype(BF16)
    r1 = x - hi.astype(F32)
    mid = r1.astype(BF16)
    lo = (r1 - mid.astype(F32)).astype(BF16)
    s = jnp.dot(tri, jnp.concatenate([hi, mid, lo], axis=1), preferred_element_type=F32)
    w = x.shape[1]
    return s[:, :w] + s[:, w:2 * w] + s[:, 2 * w:]


def _hg_chunk(q, vb, k, logf, st_ref, tri, rmask, rev):
    c, w = q.shape
    b = _cumsum_rows(tri, logf)
    row = lax.broadcasted_iota(jnp.int32, (c, w), 0)
    refs = [b[HG_SB * i + HG_SB // 2:HG_SB * i + HG_SB // 2 + 1, :] for i in range(HG_NSB)]
    rrow = jnp.concatenate([jnp.broadcast_to(r, (HG_SB, w)) for r in refs], axis=0)
    qt = (q * jnp.exp(b - rrow)).astype(BF16)
    kparts = []
    for i in range(HG_NSB):
        earlier = (row >= HG_SB * i) if rev else (row < HG_SB * (i + 1))
        kparts.append((k * jnp.exp(jnp.where(earlier, refs[i] - b, MASK_VALUE))).astype(BF16))
    kall = jnp.concatenate(kparts, axis=0)
    r = lax.dot_general(qt, kall, _NT, preferred_element_type=F32)
    rm = jnp.where(rmask, r, 0.0).astype(BF16)
    o = jnp.dot(rm, jnp.concatenate([vb] * HG_NSB, axis=0), preferred_element_type=F32)
    st = st_ref[...]
    qh = (q * jnp.exp(b)).astype(BF16)
    o = o + lax.dot_general(qh, st.astype(BF16), _NT, preferred_element_type=F32)
    bend = b[0:1, :] if rev else b[c - 1:c, :]
    kh = (k * jnp.exp(bend - b)).astype(BF16)
    st_ref[...] = st * jnp.exp(bend) + lax.dot_general(vb, kh, _TN, preferred_element_type=F32)
    return o


def _hgrn_body(q_ref, v_ref, zf_ref, zb_ref, g_ref, lb_ref, gn_ref, o_ref, of_ref, st_ref):
    t_len = q_ref.shape[0]
    nc = t_len // HG_C
    ti = lax.broadcasted_iota(jnp.int32, (HG_C, HG_C), 0)
    si = lax.broadcasted_iota(jnp.int32, (HG_C, HG_C), 1)
    tri_f = jnp.where(si <= ti, 1.0, 0.0).astype(BF16)
    tri_b = jnp.where(si >= ti, 1.0, 0.0).astype(BF16)
    tr = lax.broadcasted_iota(jnp.int32, (HG_C, HG_NSB * HG_C), 0)
    cr = lax.broadcasted_iota(jnp.int32, (HG_C, HG_NSB * HG_C), 1)
    own = (cr // HG_C) == (tr // HG_SB)
    src = cr % HG_C
    rmask_f = own & (src <= tr)
    rmask_b = own & (src >= tr)
    lb_f = lb_ref[0:1, :]
    lb_b = lb_ref[1:2, :]

    st_ref[...] = jnp.zeros_like(st_ref)

    def fwd(n, carry):
        rows = pl.ds(pl.multiple_of(n * HG_C, HG_C), HG_C)
        k, logf = _hg_gates(zf_ref[rows, :].astype(F32), lb_f)
        of_ref[rows, :] = _hg_chunk(q_ref[rows, :].astype(F32), v_ref[rows, :], k, logf,
                                    st_ref, tri_f, rmask_f, False)
        return carry

    lax.fori_loop(0, nc, fwd, 0)
    st_ref[...] = jnp.zeros_like(st_ref)

    def bwd(i, carry):
        rows = pl.ds(pl.multiple_of((nc - 1 - i) * HG_C, HG_C), HG_C)
        k, logf = _hg_gates(zb_ref[rows, :].astype(F32), lb_b)
        o = of_ref[rows, :] + _hg_chunk(q_ref[rows, :].astype(F32), v_ref[rows, :], k, logf,
                                        st_ref, tri_b, rmask_b, True)
        g = g_ref[rows, :].astype(F32)
        o = _rms_scale(o, gn_ref[...]) * (g * jax.nn.sigmoid(g))
        o_ref[rows, :] = o.astype(BF16)
        return carry

    lax.fori_loop(0, nc, bwd, 0)


def _hgrn(proj, lb, gnorm, layer, batch, t_len):
    spec = lambda c0: pl.BlockSpec((t_len, LANES), lambda b, h: (b, c0 + h))
    return pl.pallas_call(
        _hgrn_body,
        grid=(batch, HG_HEADS),
        in_specs=[spec(_COL["hq"]), spec(_COL["hi"]), spec(_COL["hzf"]), spec(_COL["hzb"]),
                  spec(_COL["hg"]),
                  pl.BlockSpec((None, 2, LANES), lambda b, h: (layer, 0, h)),
                  pl.BlockSpec((None, 1, LANES), lambda b, h: (layer, 0, h))],
        out_specs=pl.BlockSpec((t_len, LANES), lambda b, h: (b, h)),
        out_shape=jax.ShapeDtypeStruct((batch * t_len, MIX_W), BF16),
        scratch_shapes=[pltpu.VMEM((t_len, LANES), F32), pltpu.VMEM((HG_DK, HG_DK), F32)],
        compiler_params=_cparams(("parallel", "parallel")),
    )(proj, proj, proj, proj, proj, lb, gnorm)


def _conv_body(ca_ref, cb_ref, cc_ref, w_ref, bias_ref, o_ref):
    t_len = ca_ref.shape[0]
    z = cc_ref[...].astype(F32) * ca_ref[...].astype(F32)
    row = lax.broadcasted_iota(jnp.int32, z.shape, 0)
    prev = jnp.where(row == 0, 0.0, pltpu.roll(z, 1, axis=0))
    nxt = jnp.where(row == t_len - 1, 0.0, pltpu.roll(z, t_len - 1, axis=0))
    zc = prev * w_ref[0:1, :] + z * w_ref[1:2, :] + nxt * w_ref[2:3, :]
    o_ref[...] = (cb_ref[...].astype(F32) * (zc + bias_ref[...])).astype(BF16)


def _conv(proj, conv_w, conv_b, layer, batch, t_len):
    spec = lambda c0: pl.BlockSpec((t_len, CV_GROUP), lambda b, c: (b, c0 + c))
    return pl.pallas_call(
        _conv_body,
        grid=(batch, MIX_W // CV_GROUP),
        in_specs=[spec(_COL["ca"]), spec(_COL["cb"]), spec(_COL["cc"]),
                  pl.BlockSpec((None, 3, CV_GROUP), lambda b, c: (layer, 0, c)),
                  pl.BlockSpec((None, 1, CV_GROUP), lambda b, c: (layer, 0, c))],
        out_specs=pl.BlockSpec((t_len, CV_GROUP), lambda b, c: (b, c)),
        out_shape=jax.ShapeDtypeStruct((batch * t_len, MIX_W), BF16),
        compiler_params=_cparams(("parallel", "parallel")),
    )(proj, proj, proj, conv_w, conv_b)


def _na_bias_table(rpb):
    n_layers = rpb.shape[0]
    qc = np.arange(GRID_W)[:, None]
    kc = np.arange(GRID_W)[None, :]
    cs = np.clip(qc - NA_KW // 2, 0, GRID_W - NA_KW)
    col_ok = (kc >= cs) & (kc < cs + NA_KW)
    dc = np.clip(kc - qc + NA_KW - 1, 0, 2 * NA_KW - 2)
    d = np.arange(NA_KH)[:, None]
    a = np.arange(NA_KH)[None, :]
    dr = a - d + NA_KH - 1
    t = rpb.astype(F32)[:, :, dr][:, :, :, :, dc]
    t = jnp.where(jnp.asarray(col_ok), t, MASK_VALUE)
    t = t.transpose(0, 1, 2, 4, 3, 5)
    t = t.reshape(n_layers, NA_HEADS // 2, 2, NA_KH, GRID_W, NA_KH * GRID_W)
    t = t.transpose(0, 1, 3, 2, 4, 5)
    return t.reshape(n_layers, NA_HEADS // 2, NA_KH, 2 * GRID_W, NA_KH * GRID_W)


def _na_body(q_ref, k_ref, v_ref, bias_ref, o_ref):
    n_rows = q_ref.shape[0] // GRID_W
    lane = lax.broadcasted_iota(jnp.int32, (GRID_W, LANES), 1)
    first = lane < NA_DH
    m0 = jnp.where(first, NA_SCALE, 0.0)
    m1 = jnp.where(first, 0.0, NA_SCALE)
    win = NA_KH * GRID_W

    def one_row(r, carry):
        rs = jnp.clip(r - NA_KH // 2, 0, n_rows - NA_KH)
        q = q_ref[pl.ds(pl.multiple_of(r * GRID_W, GRID_W), GRID_W), :].astype(F32)
        q2 = jnp.concatenate([q * m0, q * m1], axis=0).astype(BF16)
        krows = pl.ds(pl.multiple_of(rs * GRID_W, GRID_W), win)
        s = lax.dot_general(q2, k_ref[krows, :], _NT, preferred_element_type=F32)
        s = s + bias_ref[r - rs]
        p = jnp.exp(s - jnp.max(s, axis=-1, keepdims=True))
        l = jnp.sum(p, axis=-1, keepdims=True)
        o2 = jnp.dot(p.astype(BF16), v_ref[krows, :], preferred_element_type=F32) / l
        o = jnp.where(first, o2[:GRID_W, :], o2[GRID_W:, :])
        o_ref[pl.ds(pl.multiple_of(r * GRID_W, GRID_W), GRID_W), :] = o.astype(BF16)
        return carry

    lax.fori_loop(0, n_rows, one_row, 0)


def _natten(proj, bias, layer, batch, t_len):
    assert t_len % GRID_W == 0 and t_len // GRID_W >= NA_KH
    spec = lambda c0: pl.BlockSpec((t_len, LANES), lambda p, b: (b, c0 + p))
    return pl.pallas_call(
        _na_body,
        grid=(NA_HEADS // 2, batch),
        in_specs=[spec(_COL["nq"]), spec(_COL["nk"]), spec(_COL["nv"]),
                  pl.BlockSpec((None, None, NA_KH, 2 * GRID_W, NA_KH * GRID_W),
                               lambda p, b: (layer, p, 0, 0, 0))],
        out_specs=pl.BlockSpec((t_len, LANES), lambda p, b: (b, p)),
        out_shape=jax.ShapeDtypeStruct((batch * t_len, MIX_W), BF16),
        compiler_params=_cparams(("parallel", "parallel")),
    )(proj, proj, proj, bias)


def _mix_body(x_ref, ghg_ref, gcv_ref, gna_ref, ohg_ref, ocv_ref, ona_ref,
              whg_ref, wcv_ref, wna_ref, wout_ref, o_ref):
    def branch(g_ref, a_ref, w_ref):
        y = jnp.dot(a_ref[...], w_ref[...], preferred_element_type=F32)
        return jax.nn.sigmoid(g_ref[...].astype(F32)) * y

    m = branch(ghg_ref, ohg_ref, whg_ref) + branch(gcv_ref, ocv_ref, wcv_ref) \
        + branch(gna_ref, ona_ref, wna_ref)
    o_ref[...] = x_ref[...] + jnp.dot(m.astype(BF16), wout_ref[...], preferred_element_type=F32)


def _mix_out(x, proj, o_hg, o_cv, o_na, w_hg, w_cv, w_na, w_out, layer):
    n, d = x.shape
    gate = lambda c0: pl.BlockSpec((TM_FFN, d), lambda i: (i, c0 * LANES // d))
    act = pl.BlockSpec((TM_FFN, MIX_W), lambda i: (i, 0))
    wmix = pl.BlockSpec((None, MIX_W, d), lambda i: (layer, 0, 0))
    return pl.pallas_call(
        _mix_body,
        grid=(n // TM_FFN,),
        in_specs=[pl.BlockSpec((TM_FFN, d), lambda i: (i, 0)),
                  gate(_COL["g_hg"]), gate(_COL["g_cv"]), gate(_COL["g_na"]),
                  act, act, act, wmix, wmix, wmix,
                  pl.BlockSpec((None, d, d), lambda i: (layer, 0, 0))],
        out_specs=pl.BlockSpec((TM_FFN, d), lambda i: (i, 0)),
        out_shape=jax.ShapeDtypeStruct((n, d), F32),
        compiler_params=_cparams(("parallel",)),
    )(x, proj, proj, proj, o_hg, o_cv, o_na, w_hg, w_cv, w_na, w_out)


def _norm_body(x_ref, g_ref, o_ref):
    o_ref[...] = _rms_scale(x_ref[...], g_ref[...])


def _final_norm(x, gain):
    n, d = x.shape
    return pl.pallas_call(
        _norm_body,
        grid=(n // TM_FFN,),
        in_specs=[pl.BlockSpec((TM_FFN, d), lambda i: (i, 0)), pl.BlockSpec((1, d), lambda i: (0, 0))],
        out_specs=pl.BlockSpec((TM_FFN, d), lambda i: (i, 0)),
        out_shape=jax.ShapeDtypeStruct((n, d), F32),
        compiler_params=_cparams(("parallel",)),
    )(x, gain)


def _lower_bounds(lb_logits):
    p = jax.nn.softmax(lb_logits.astype(F32), axis=1)
    return jnp.cumsum(p, axis=1) - p[:, :1]


def _trunk(x3, p):
    batch, t_len, d = x3.shape
    assert d % LANES == 0 and t_len % HG_C == 0 and (batch * t_len) % TM_FFN == 0
    x = x3.reshape(batch * t_len, d)
    for l in range(p["n_layers"]):
        x = _ffn(x, p["ffn1_norm"], p["ffn1_w_gu"], p["ffn1_w_down"], l)
        proj = _inproj(x, p["mix_norm"], p["w_in"], l)
        o_hg = _hgrn(proj, p["lb"], p["hg_out_norm"], l, batch, t_len)
        o_cv = _conv(proj, p["conv_w"], p["conv_b"], l, batch, t_len)
        o_na = _natten(proj, p["na_bias"], l, batch, t_len)
        x = _mix_out(x, proj, o_hg, o_cv, o_na, p["w_hg_out"], p["w_cv_out"], p["w_na_out"],
                     p["w_out"], l)
        x = _ffn(x, p["ffn2_norm"], p["ffn2_w_gu"], p["ffn2_w_down"], l)
    return _final_norm(x, p["final_norm"]).reshape(batch, t_len, d)


def kernel(x_prompt, x_sample, ffn1_norm, ffn1_w_gu, ffn1_w_down, mix_norm, w_in, hg_lb_logits, hg_out_norm, w_hg_out, conv_w, conv_b, w_cv_out, na_rpb, w_na_out, w_out, ffn2_norm, ffn2_w_gu, ffn2_w_down, final_norm):
    n_layers, d, n_in = w_in.shape
    assert n_in == N_IN
    gates0 = _COL_GATES_SRC * LANES
    row = lambda a: a.astype(F32)[:, None, :]
    p = dict(
        n_layers=n_layers,
        ffn1_norm=row(ffn1_norm), ffn1_w_gu=ffn1_w_gu.astype(BF16), ffn1_w_down=ffn1_w_down.astype(BF16),
        ffn2_norm=row(ffn2_norm), ffn2_w_gu=ffn2_w_gu.astype(BF16), ffn2_w_down=ffn2_w_down.astype(BF16),
        mix_norm=row(mix_norm),
        w_in=jnp.concatenate([w_in[..., gates0:], w_in[..., :gates0]], axis=-1).astype(BF16),
        lb=_lower_bounds(hg_lb_logits).transpose(1, 0, 2),
        hg_out_norm=row(hg_out_norm),
        w_hg_out=w_hg_out.astype(BF16), w_cv_out=w_cv_out.astype(BF16), w_na_out=w_na_out.astype(BF16),
        conv_w=conv_w.astype(F32), conv_b=row(conv_b),
        na_bias=_na_bias_table(na_rpb),
        w_out=w_out.astype(BF16),
        final_norm=final_norm.astype(F32)[None, :],
    )
    return _trunk(x_prompt, p), _trunk(x_sample, p)


_COL_GATES_SRC = 44
```

```python
import functools

import numpy as np
import jax
import jax.numpy as jnp
from jax import lax
from jax.experimental import pallas as pl
from jax.experimental.pallas import tpu as pltpu

F32 = jnp.float32
BF16 = jnp.bfloat16

GRID_W = 64
RMS_EPS = 1e-6
F_FLOOR = 1e-30
MASK_VALUE = -1e30
HG_HEADS, HG_DK = 4, 128
CV_GROUP = 128
NA_HEADS, NA_DH = 8, 64
NA_KH, NA_KW = 8, 16
NA_SCALE = NA_DH ** -0.5
MIX_W = HG_HEADS * HG_DK
GATES_SRC_COL = 44

LANES = 128
MXU_N = 256
VMEM_LIMIT = 56 * 1024 * 1024

TM = 512
N_CHUNK = 2 * MXU_N
HG_C = 64
HG_SB = 16
HG_NSB = HG_C // HG_SB
HG_UNROLL = 4
NA_UNROLL = 4

_COL = dict(g_hg=0, g_cv=8, g_na=16, hq=24, hi=28, hzf=32, hzb=36, hg=40,
            ca=44, cb=48, cc=52, nq=56, nk=60, nv=64)
N_IN = 68 * LANES

_NT = (((1,), (1,)), ((), ()))
_TN = (((0,), (0,)), ((), ()))


def _cparams(sem):
    return pltpu.CompilerParams(dimension_semantics=sem, vmem_limit_bytes=VMEM_LIMIT)


def _resident(block_shape, index_map):
    return pl.BlockSpec(block_shape, index_map, pipeline_mode=pl.Buffered(1))


def _rms_scale(x, gain):
    return x * lax.rsqrt(jnp.mean(x * x, axis=-1, keepdims=True) + RMS_EPS) * gain


def _col_chunks(n):
    return [(c, min(c + N_CHUNK, n)) for c in range(0, n, N_CHUNK)]


def _ffn_body(x_ref, g_ref, wg_ref, wu_ref, wd_ref, *rest, final):
    o_ref = rest[-1]
    x = x_ref[...]
    xn = _rms_scale(x, g_ref[...]).astype(BF16)

    def hidden(c0, c1):
        a = jnp.dot(xn, wg_ref[:, c0:c1], preferred_element_type=F32)
        b = jnp.dot(xn, wu_ref[:, c0:c1], preferred_element_type=F32)
        return (a * jax.nn.sigmoid(a) * b).astype(BF16)

    chunks = _col_chunks(wd_ref.shape[0])
    acc = None
    h = hidden(*chunks[0])
    for idx, (c0, c1) in enumerate(chunks):
        h_next = hidden(*chunks[idx + 1]) if idx + 1 < len(chunks) else None
        part = jnp.dot(h, wd_ref[c0:c1, :], preferred_element_type=F32)
        acc = part if acc is None else acc + part
        h = h_next
    y = x + 0.5 * acc
    if final:
        y = _rms_scale(y, rest[0][...])
    o_ref[...] = y


def _ffn(x, gain, w_gu, w_down, layer, final_gain=None):
    n, d = x.shape
    d_ff = w_down.shape[1]
    final = final_gain is not None
    in_specs = [
        pl.BlockSpec((TM, d), lambda i: (i, 0)),
        _resident((None, 1, d), lambda i: (layer, 0, 0)),
        _resident((None, d, d_ff), lambda i: (layer, 0, 0)),
        _resident((None, d, d_ff), lambda i: (layer, 0, 1)),
        _resident((None, d_ff, d), lambda i: (layer, 0, 0)),
    ]
    args = [x, gain, w_gu, w_gu, w_down]
    if final:
        in_specs.append(_resident((1, d), lambda i: (0, 0)))
        args.append(final_gain)
    return pl.pallas_call(
        functools.partial(_ffn_body, final=final),
        grid=(n // TM,),
        in_specs=in_specs,
        out_specs=pl.BlockSpec((TM, d), lambda i: (i, 0)),
        out_shape=jax.ShapeDtypeStruct((n, d), F32),
        compiler_params=_cparams(("parallel",)),
    )(*args)


def _inproj_body(x_ref, g_ref, w_ref, o_ref):
    xn = _rms_scale(x_ref[...], g_ref[...]).astype(BF16)
    for c0, c1 in _col_chunks(w_ref.shape[1]):
        o_ref[:, c0:c1] = jnp.dot(xn, w_ref[:, c0:c1], preferred_element_type=F32).astype(BF16)


def _inproj(x, gain, w_in, layer):
    n, d = x.shape
    n_in = w_in.shape[-1]
    return pl.pallas_call(
        _inproj_body,
        grid=(n // TM,),
        in_specs=[
            pl.BlockSpec((TM, d), lambda i: (i, 0)),
            _resident((None, 1, d), lambda i: (layer, 0, 0)),
            _resident((None, d, n_in), lambda i: (layer, 0, 0)),
        ],
        out_specs=pl.BlockSpec((TM, n_in), lambda i: (i, 0)),
        out_shape=jax.ShapeDtypeStruct((n, n_in), BF16),
        compiler_params=_cparams(("parallel",)),
    )(x, gain, w_in)


def _cumsum_rows(tri, x):
    hi = x.astype(BF16)
    lo = (x - hi.astype(F32)).astype(BF16)
    s = jnp.dot(tri, jnp.concatenate([hi, lo], axis=1), preferred_element_type=F32)
    w = x.shape[1]
    return s[:, :w] + s[:, w:]


def _per_block(vecs):
    return jnp.concatenate([jnp.broadcast_to(v, (HG_SB, v.shape[1])) for v in vecs], axis=0)


def _key_variants(kt, refs, rev):
    blocks = [kt[HG_SB * j:HG_SB * (j + 1), :] for j in range(HG_NSB)]
    zero = jnp.zeros(blocks[0].shape, BF16)
    parts = []
    for i in range(HG_NSB):
        for j in range(HG_NSB):
            if j == i:
                parts.append(blocks[j].astype(BF16))
            elif (j > i) if rev else (j < i):
                parts.append((blocks[j] * jnp.exp(refs[i] - refs[j])).astype(BF16))
            else:
                parts.append(zero)
    return jnp.concatenate(parts, axis=0)


def _hgrn_body(q_ref, v_ref, zf_ref, zb_ref, g_ref, lb_ref, gn_ref, o_ref,
               ktf_ref, ktb_ref, qtf_ref, qtb_ref, rvf_ref, rvb_ref, s_ref, stf_ref, stb_ref,
               gk_ref, gb_ref, rm_ref, oi_ref):
    t_len = q_ref.shape[0]
    nc = t_len // HG_C
    w = HG_DK
    mid = [HG_SB * i + HG_SB // 2 for i in range(HG_NSB)]
    ti = lax.broadcasted_iota(jnp.int32, (HG_C, HG_C), 0)
    si = lax.broadcasted_iota(jnp.int32, (HG_C, HG_C), 1)
    tri_f = jnp.where(si <= ti, 1.0, 0.0).astype(BF16)
    tri_b = jnp.where(si >= ti, 1.0, 0.0).astype(BF16)
    tr = lax.broadcasted_iota(jnp.int32, (HG_C, HG_NSB * HG_C), 0)
    cr = lax.broadcasted_iota(jnp.int32, (HG_C, HG_NSB * HG_C), 1)
    own = (cr // HG_C) == (tr // HG_SB)
    src = cr % HG_C
    rmask_f = own & (src <= tr)
    rmask_b = own & (src >= tr)
    lb_f = lb_ref[0:1, :]
    lb_b = lb_ref[1:2, :]

    stf_ref[...] = jnp.zeros_like(stf_ref)
    stb_ref[...] = jnp.zeros_like(stb_ref)

    def chunk_rows(c):
        return pl.ds(pl.multiple_of(c * HG_C, HG_C), HG_C)

    n_blk = nc // HG_UNROLL
    n_cd = 2 * HG_UNROLL
    side = {False: (zf_ref, lb_f, tri_f, ktf_ref, qtf_ref, rvf_ref),
            True: (zb_ref, lb_b, tri_b, ktb_ref, qtb_ref, rvb_ref)}

    def block_work(blk):
        work = []
        for u in range(HG_UNROLL):
            n = blk * HG_UNROLL + u
            work += [(False, n), (True, nc - 1 - n)]
        return work

    def gates(blk):
        out = []
        for rev, c in block_work(blk):
            z_ref, lb, tri = side[rev][:3]
            f = lb + (1.0 - lb) * jax.nn.sigmoid(z_ref[chunk_rows(c), :].astype(F32))
            out.append((1.0 - f, _cumsum_rows(tri, jnp.log(jnp.maximum(f, F_FLOOR)))))
        return out

    def park(refs, slot, items):
        for idx, vals in enumerate(items):
            for ref, val in zip(refs, vals):
                ref[slot + idx] = val

    park((gk_ref, gb_ref), 0, gates(0))

    def states(i, slot):
        steps = []
        for idx, (rev, c) in enumerate(block_work(i)):
            k, b = gk_ref[slot + idx], gb_ref[slot + idx]
            rows = chunk_rows(c)
            refs = [b[m:m + 1, :] for m in mid]
            bend = b[0:1, :] if rev else b[HG_C - 1:HG_C, :]
            d = _per_block(refs) - b
            kt = k * jnp.exp(d)
            qt = (q_ref[rows, :].astype(F32) * jnp.exp(-d)).astype(BF16)
            kh = (kt * _per_block([jnp.exp(bend - r) for r in refs])).astype(BF16)
            upd = lax.dot_general(v_ref[rows, :], kh, _TN, preferred_element_type=F32)
            rv = jnp.concatenate(refs + refs, axis=0)
            steps.append((rev, c, kt, qt, rv, jnp.exp(bend), upd))
        return steps

    def commit_states(steps):
        st = {False: stf_ref[...], True: stb_ref[...]}
        for rev, c, kt, qt, rv, dec, upd in steps:
            kt_ref, qt_ref, rv_ref = side[rev][3:]
            rows = chunk_rows(c)
            kt_ref[rows, :] = kt
            qt_ref[rows, :] = qt
            rv_ref[c] = rv
            s_ref[c, :, (w if rev else 0):(2 * w if rev else w)] = st[rev].astype(BF16)
            st[rev] = st[rev] * dec + upd
        stf_ref[...] = st[False]
        stb_ref[...] = st[True]

    def pass1(i, carry):
        cur = (i & 1) * n_cd
        steps = states(i, cur)
        fresh = gates(i + 1)
        commit_states(steps)
        park((gk_ref, gb_ref), n_cd - cur, fresh)
        return carry

    lax.fori_loop(0, n_blk - 1, pass1, 0)
    commit_states(states(n_blk - 1, ((n_blk - 1) & 1) * n_cd))

    def scores(blk):
        pre = []
        for u in range(HG_UNROLL):
            c = blk * HG_UNROLL + u
            rows = chunk_rows(c)
            rs, qhs = [], []
            for rev in (False, True):
                kt_ref, qt_ref, rv_ref = side[rev][3:]
                rv = rv_ref[c]
                refs = [rv[j:j + 1, :] for j in range(HG_NSB)]
                qt = qt_ref[rows, :]
                kall = _key_variants(kt_ref[rows, :], refs, rev)
                rs.append(lax.dot_general(qt, kall, _NT, preferred_element_type=F32))
                qhs.append(qt.astype(F32) * _per_block([jnp.exp(r) for r in refs]))
            pre.append((c, rs, jnp.concatenate(qhs, axis=1).astype(BF16)))
        out = []
        for c, (r_f, r_b), qh in pre:
            oi = lax.dot_general(qh, s_ref[c], _NT, preferred_element_type=F32)
            rm = (jnp.where(rmask_f, r_f, 0.0) + jnp.where(rmask_b, r_b, 0.0)).astype(BF16)
            out.append((rm, oi))
        return out

    park((rm_ref, oi_ref), 0, scores(0))

    def outputs(i, slot):
        outs = []
        for u in range(HG_UNROLL):
            rows = chunk_rows(i * HG_UNROLL + u)
            v4 = jnp.concatenate([v_ref[rows, :]] * HG_NSB, axis=0)
            o = oi_ref[slot + u] + jnp.dot(rm_ref[slot + u], v4, preferred_element_type=F32)
            g = g_ref[rows, :].astype(F32)
            o = _rms_scale(o, gn_ref[...]) * (g * jax.nn.sigmoid(g))
            outs.append((rows, o.astype(BF16)))
        return outs

    def commit_outputs(outs):
        for rows, o in outs:
            o_ref[rows, :] = o

    def pass2(i, carry):
        cur = (i & 1) * HG_UNROLL
        outs = outputs(i, cur)
        fresh = scores(i + 1)
        commit_outputs(outs)
        park((rm_ref, oi_ref), HG_UNROLL - cur, fresh)
        return carry

    lax.fori_loop(0, n_blk - 1, pass2, 0)
    commit_outputs(outputs(n_blk - 1, ((n_blk - 1) & 1) * HG_UNROLL))


def _hgrn(proj, lb, gnorm, layer, batch, t_len):
    assert (t_len // HG_C) % HG_UNROLL == 0
    nc = t_len // HG_C
    spec = lambda c0: pl.BlockSpec((t_len, LANES), lambda b, h: (b, c0 + h))
    seq = lambda dt: pltpu.VMEM((t_len, HG_DK), dt)
    rvec = pltpu.VMEM((nc, 2 * HG_NSB, HG_DK), F32)
    return pl.pallas_call(
        _hgrn_body,
        grid=(batch, HG_HEADS),
        in_specs=[spec(_COL["hq"]), spec(_COL["hi"]), spec(_COL["hzf"]), spec(_COL["hzb"]),
                  spec(_COL["hg"]),
                  pl.BlockSpec((None, 2, LANES), lambda b, h: (layer, 0, h)),
                  pl.BlockSpec((None, 1, LANES), lambda b, h: (layer, 0, h))],
        out_specs=pl.BlockSpec((t_len, LANES), lambda b, h: (b, h)),
        out_shape=jax.ShapeDtypeStruct((batch * t_len, MIX_W), BF16),
        scratch_shapes=[seq(F32), seq(F32), seq(BF16), seq(BF16), rvec, rvec,
                        pltpu.VMEM((nc, HG_DK, 2 * HG_DK), BF16),
                        pltpu.VMEM((HG_DK, HG_DK), F32), pltpu.VMEM((HG_DK, HG_DK), F32),
                        pltpu.VMEM((4 * HG_UNROLL, HG_C, HG_DK), F32),
                        pltpu.VMEM((4 * HG_UNROLL, HG_C, HG_DK), F32),
                        pltpu.VMEM((2 * HG_UNROLL, HG_C, HG_NSB * HG_C), BF16),
                        pltpu.VMEM((2 * HG_UNROLL, HG_C, HG_DK), F32)],
        compiler_params=_cparams(("parallel", "parallel")),
    )(proj, proj, proj, proj, proj, lb, gnorm)


def _conv_body(ca_ref, cb_ref, cc_ref, w_ref, bias_ref, o_ref):
    t_len = ca_ref.shape[0]
    z = cc_ref[...].astype(F32) * ca_ref[...].astype(F32)
    row = lax.broadcasted_iota(jnp.int32, z.shape, 0)
    prev = jnp.where(row == 0, 0.0, pltpu.roll(z, 1, axis=0))
    nxt = jnp.where(row == t_len - 1, 0.0, pltpu.roll(z, t_len - 1, axis=0))
    zc = prev * w_ref[0:1, :] + z * w_ref[1:2, :] + nxt * w_ref[2:3, :]
    o_ref[...] = (cb_ref[...].astype(F32) * (zc + bias_ref[...])).astype(BF16)


def _conv(proj, conv_w, conv_b, layer, batch, t_len):
    spec = lambda c0: pl.BlockSpec((t_len, CV_GROUP), lambda b, c: (b, c0 + c))
    return pl.pallas_call(
        _conv_body,
        grid=(batch, MIX_W // CV_GROUP),
        in_specs=[spec(_COL["ca"]), spec(_COL["cb"]), spec(_COL["cc"]),
                  pl.BlockSpec((None, 3, CV_GROUP), lambda b, c: (layer, 0, c)),
                  pl.BlockSpec((None, 1, CV_GROUP), lambda b, c: (layer, 0, c))],
        out_specs=pl.BlockSpec((t_len, CV_GROUP), lambda b, c: (b, c)),
        out_shape=jax.ShapeDtypeStruct((batch * t_len, MIX_W), BF16),
        compiler_params=_cparams(("parallel", "parallel")),
    )(proj, proj, proj, conv_w, conv_b)


def _na_bias_table(rpb):
    n_layers = rpb.shape[0]
    qc = np.arange(GRID_W)[:, None]
    kc = np.arange(GRID_W)[None, :]
    cs = np.clip(qc - NA_KW // 2, 0, GRID_W - NA_KW)
    col_ok = (kc >= cs) & (kc < cs + NA_KW)
    dc = np.clip(kc - qc + NA_KW - 1, 0, 2 * NA_KW - 2)
    d = np.arange(NA_KH)[:, None]
    a = np.arange(NA_KH)[None, :]
    dr = a - d + NA_KH - 1
    t = rpb.astype(F32)[:, :, dr][:, :, :, :, dc]
    t = jnp.where(jnp.asarray(col_ok), t, MASK_VALUE)
    t = t.transpose(0, 1, 2, 4, 3, 5)
    t = t.reshape(n_layers, NA_HEADS // 2, 2, NA_KH, GRID_W, NA_KH * GRID_W)
    t = t.transpose(0, 1, 3, 2, 4, 5)
    return t.reshape(n_layers, NA_HEADS // 2, NA_KH, 2 * GRID_W, NA_KH * GRID_W)


def _na_body(q_ref, k_ref, v_ref, bias_ref, o_ref):
    n_rows = q_ref.shape[0] // GRID_W
    lane = lax.broadcasted_iota(jnp.int32, (GRID_W, LANES), 1)
    first = lane < NA_DH
    m0 = jnp.where(first, NA_SCALE, 0.0)
    m1 = jnp.where(first, 0.0, NA_SCALE)
    win = NA_KH * GRID_W

    def window(r):
        rs = jnp.clip(r - NA_KH // 2, 0, n_rows - NA_KH)
        return rs, pl.ds(pl.multiple_of(rs * GRID_W, GRID_W), win)

    def logits(r):
        rs, krows = window(r)
        q = q_ref[pl.ds(pl.multiple_of(r * GRID_W, GRID_W), GRID_W), :].astype(F32)
        q2 = jnp.concatenate([q * m0, q * m1], axis=0).astype(BF16)
        return lax.dot_general(q2, k_ref[krows, :], _NT, preferred_element_type=F32) \
            + bias_ref[r - rs]

    def attend(s, vw):
        p = jnp.exp(s - jnp.max(s, axis=-1, keepdims=True))
        l = jnp.sum(p, axis=-1, keepdims=True)
        o2 = jnp.dot(p.astype(BF16), vw, preferred_element_type=F32) / l
        return jnp.where(first, o2[:GRID_W, :], o2[GRID_W:, :]).astype(BF16)

    def rows_step(i, carry):
        rows = [i * NA_UNROLL + u for u in range(NA_UNROLL)]
        scores = [logits(r) for r in rows]
        outs = [attend(s, v_ref[window(r)[1], :]) for s, r in zip(scores, rows)]
        for r, o in zip(rows, outs):
            o_ref[pl.ds(pl.multiple_of(r * GRID_W, GRID_W), GRID_W), :] = o
        return carry

    lax.fori_loop(0, n_rows // NA_UNROLL, rows_step, 0)


def _natten(proj, bias, layer, batch, t_len):
    assert t_len % GRID_W == 0 and t_len // GRID_W >= NA_KH
    assert (t_len // GRID_W) % NA_UNROLL == 0
    spec = lambda c0: pl.BlockSpec((t_len, LANES), lambda p, b: (b, c0 + p))
    return pl.pallas_call(
        _na_body,
        grid=(NA_HEADS // 2, batch),
        in_specs=[spec(_COL["nq"]), spec(_COL["nk"]), spec(_COL["nv"]),
                  pl.BlockSpec((None, None, NA_KH, 2 * GRID_W, NA_KH * GRID_W),
                               lambda p, b: (layer, p, 0, 0, 0))],
        out_specs=pl.BlockSpec((t_len, LANES), lambda p, b: (b, p)),
        out_shape=jax.ShapeDtypeStruct((batch * t_len, MIX_W), BF16),
        compiler_params=_cparams(("parallel", "parallel")),
    )(proj, proj, proj, bias)


def _mix_body(x_ref, ghg_ref, gcv_ref, gna_ref, ohg_ref, ocv_ref, ona_ref,
              whg_ref, wcv_ref, wna_ref, wout_ref, o_ref):
    ys = [jnp.dot(a_ref[...], w_ref[...], preferred_element_type=F32)
          for a_ref, w_ref in ((ohg_ref, whg_ref), (ocv_ref, wcv_ref), (ona_ref, wna_ref))]
    m = sum(jax.nn.sigmoid(g_ref[...].astype(F32)) * y
            for g_ref, y in zip((ghg_ref, gcv_ref, gna_ref), ys))
    o_ref[...] = x_ref[...] + jnp.dot(m.astype(BF16), wout_ref[...], preferred_element_type=F32)


def _mix_out(x, proj, o_hg, o_cv, o_na, w_hg, w_cv, w_na, w_out, layer):
    n, d = x.shape
    gate = lambda c0: pl.BlockSpec((TM, d), lambda i: (i, c0 * LANES // d))
    act = pl.BlockSpec((TM, MIX_W), lambda i: (i, 0))
    wmix = _resident((None, MIX_W, d), lambda i: (layer, 0, 0))
    return pl.pallas_call(
        _mix_body,
        grid=(n // TM,),
        in_specs=[pl.BlockSpec((TM, d), lambda i: (i, 0)),
                  gate(_COL["g_hg"]), gate(_COL["g_cv"]), gate(_COL["g_na"]),
                  act, act, act, wmix, wmix, wmix,
                  _resident((None, d, d), lambda i: (layer, 0, 0))],
        out_specs=pl.BlockSpec((TM, d), lambda i: (i, 0)),
        out_shape=jax.ShapeDtypeStruct((n, d), F32),
        compiler_params=_cparams(("parallel",)),
    )(x, proj, proj, proj, o_hg, o_cv, o_na, w_hg, w_cv, w_na, w_out)


def _lower_bounds(lb_logits):
    p = jax.nn.softmax(lb_logits.astype(F32), axis=1)
    return jnp.cumsum(p, axis=1) - p[:, :1]


def _trunk(x3, p):
    batch, t_len, d = x3.shape
    assert d % LANES == 0 and t_len % HG_C == 0 and (batch * t_len) % TM == 0
    n_layers = p["n_layers"]
    x = x3.reshape(batch * t_len, d)
    for l in range(n_layers):
        x = _ffn(x, p["ffn1_norm"], p["ffn1_w_gu"], p["ffn1_w_down"], l)
        proj = _inproj(x, p["mix_norm"], p["w_in"], l)
        o_hg = _hgrn(proj, p["lb"], p["hg_out_norm"], l, batch, t_len)
        o_cv = _conv(proj, p["conv_w"], p["conv_b"], l, batch, t_len)
        o_na = _natten(proj, p["na_bias"], l, batch, t_len)
        x = _mix_out(x, proj, o_hg, o_cv, o_na, p["w_hg_out"], p["w_cv_out"], p["w_na_out"],
                     p["w_out"], l)
        x = _ffn(x, p["ffn2_norm"], p["ffn2_w_gu"], p["ffn2_w_down"], l,
                 final_gain=p["final_norm"] if l == n_layers - 1 else None)
    return x.reshape(batch, t_len, d)


def kernel(x_prompt, x_sample, ffn1_norm, ffn1_w_gu, ffn1_w_down, mix_norm, w_in, hg_lb_logits, hg_out_norm, w_hg_out, conv_w, conv_b, w_cv_out, na_rpb, w_na_out, w_out, ffn2_norm, ffn2_w_gu, ffn2_w_down, final_norm):
    n_layers, d, n_in = w_in.shape
    assert n_in == N_IN
    gates0 = GATES_SRC_COL * LANES
    row = lambda a: a.astype(F32)[:, None, :]
    p = dict(
        n_layers=n_layers,
        ffn1_norm=row(ffn1_norm), ffn1_w_gu=ffn1_w_gu.astype(BF16), ffn1_w_down=ffn1_w_down.astype(BF16),
        ffn2_norm=row(ffn2_norm), ffn2_w_gu=ffn2_w_gu.astype(BF16), ffn2_w_down=ffn2_w_down.astype(BF16),
        mix_norm=row(mix_norm),
        w_in=jnp.concatenate([w_in[..., gates0:], w_in[..., :gates0]], axis=-1).astype(BF16),
        lb=_lower_bounds(hg_lb_logits).transpose(1, 0, 2),
        hg_out_norm=row(hg_out_norm),
        w_hg_out=w_hg_out.astype(BF16), w_cv_out=w_cv_out.astype(BF16), w_na_out=w_na_out.astype(BF16),
        conv_w=conv_w.astype(F32), conv_b=row(conv_b),
        na_bias=_na_bias_table(na_rpb),
        w_out=w_out.astype(BF16),
        final_norm=final_norm.astype(F32)[None, :],
    )
    return _trunk(x_prompt, p), _trunk(x_sample, p)
```

```python
import functools

import numpy as np
import jax
import jax.numpy as jnp
from jax import lax
from jax.experimental import pallas as pl
from jax.experimental.pallas import tpu as pltpu

F32 = jnp.float32
BF16 = jnp.bfloat16

GRID_W = 64
RMS_EPS = 1e-6
F_FLOOR = 1e-30
MASK_VALUE = -1e30
HG_HEADS, HG_DK = 4, 128
CV_GROUP = 128
NA_HEADS, NA_DH = 8, 64
NA_KH, NA_KW = 8, 16
NA_SCALE = NA_DH ** -0.5
MIX_W = HG_HEADS * HG_DK
GATES_SRC_COL = 44

LANES = 128
SUBLANES = 8
MXU_N = 256
VMEM_LIMIT = 56 * 1024 * 1024

TM = 512
N_CHUNK = 2 * MXU_N
HG_C = 64
HG_SB = 16
HG_NSB = HG_C // HG_SB
HG_UNROLL = 8
NA_UNROLL = 8

_COL = dict(g_hg=0, g_cv=8, g_na=16, hq=24, hi=28, hzf=32, hzb=36, hg=40,
            ca=44, cb=48, cc=52, nq=56, nk=60, nv=64)
N_IN = 68 * LANES

_NT = (((1,), (1,)), ((), ()))
_TN = (((0,), (0,)), ((), ()))


def _cparams(sem):
    return pltpu.CompilerParams(dimension_semantics=sem, vmem_limit_bytes=VMEM_LIMIT)


def _resident(block_shape, index_map):
    return pl.BlockSpec(block_shape, index_map, pipeline_mode=pl.Buffered(1))


def _rms_scale(x, gain):
    return x * lax.rsqrt(jnp.mean(x * x, axis=-1, keepdims=True) + RMS_EPS) * gain


def _col_chunks(n):
    return [(c, min(c + N_CHUNK, n)) for c in range(0, n, N_CHUNK)]


def _ffn_body(x_ref, g_ref, wg_ref, wu_ref, wd_ref, *rest, final):
    o_ref = rest[-1]
    x = x_ref[...]
    xn = _rms_scale(x, g_ref[...]).astype(BF16)

    def hidden(c0, c1):
        a = jnp.dot(xn, wg_ref[:, c0:c1], preferred_element_type=F32)
        b = jnp.dot(xn, wu_ref[:, c0:c1], preferred_element_type=F32)
        return (a * jax.nn.sigmoid(a) * b).astype(BF16)

    chunks = _col_chunks(wd_ref.shape[0])
    acc = None
    h = hidden(*chunks[0])
    for idx, (c0, c1) in enumerate(chunks):
        h_next = hidden(*chunks[idx + 1]) if idx + 1 < len(chunks) else None
        part = jnp.dot(h, wd_ref[c0:c1, :], preferred_element_type=F32)
        acc = part if acc is None else acc + part
        h = h_next
    y = x + 0.5 * acc
    if final:
        y = _rms_scale(y, rest[0][...])
    o_ref[...] = y


def _ffn(x, gain, w_gu, w_down, layer, final_gain=None):
    n, d = x.shape
    d_ff = w_down.shape[1]
    final = final_gain is not None
    in_specs = [
        pl.BlockSpec((TM, d), lambda i: (i, 0)),
        _resident((None, 1, d), lambda i: (layer, 0, 0)),
        _resident((None, d, d_ff), lambda i: (layer, 0, 0)),
        _resident((None, d, d_ff), lambda i: (layer, 0, 1)),
        _resident((None, d_ff, d), lambda i: (layer, 0, 0)),
    ]
    args = [x, gain, w_gu, w_gu, w_down]
    if final:
        in_specs.append(_resident((1, d), lambda i: (0, 0)))
        args.append(final_gain)
    return pl.pallas_call(
        functools.partial(_ffn_body, final=final),
        grid=(n // TM,),
        in_specs=in_specs,
        out_specs=pl.BlockSpec((TM, d), lambda i: (i, 0)),
        out_shape=jax.ShapeDtypeStruct((n, d), F32),
        compiler_params=_cparams(("parallel",)),
    )(*args)


def _inproj_body(x_ref, g_ref, w_ref, o_ref):
    xn = _rms_scale(x_ref[...], g_ref[...]).astype(BF16)
    for c0, c1 in _col_chunks(w_ref.shape[1]):
        o_ref[:, c0:c1] = jnp.dot(xn, w_ref[:, c0:c1], preferred_element_type=F32).astype(BF16)


def _inproj(x, gain, w_in, layer):
    n, d = x.shape
    n_in = w_in.shape[-1]
    return pl.pallas_call(
        _inproj_body,
        grid=(n // TM,),
        in_specs=[
            pl.BlockSpec((TM, d), lambda i: (i, 0)),
            _resident((None, 1, d), lambda i: (layer, 0, 0)),
            _resident((None, d, n_in), lambda i: (layer, 0, 0)),
        ],
        out_specs=pl.BlockSpec((TM, n_in), lambda i: (i, 0)),
        out_shape=jax.ShapeDtypeStruct((n, n_in), BF16),
        compiler_params=_cparams(("parallel",)),
    )(x, gain, w_in)


def _cumsum_rows(tri, x):
    hi = x.astype(BF16)
    lo = (x - hi.astype(F32)).astype(BF16)
    s = jnp.dot(tri, jnp.concatenate([hi, lo], axis=1), preferred_element_type=F32)
    w = x.shape[1]
    return s[:, :w] + s[:, w:]


def _per_block(vecs):
    return jnp.concatenate([jnp.broadcast_to(v, (HG_SB, v.shape[1])) for v in vecs], axis=0)


def _key_variants(kt, refs, rev):
    blocks = [kt[HG_SB * j:HG_SB * (j + 1), :] for j in range(HG_NSB)]
    zero = jnp.zeros(blocks[0].shape, BF16)
    parts = []
    for i in range(HG_NSB):
        for j in range(HG_NSB):
            if j == i:
                parts.append(blocks[j].astype(BF16))
            elif (j > i) if rev else (j < i):
                parts.append((blocks[j] * jnp.exp(refs[i] - refs[j])).astype(BF16))
            else:
                parts.append(zero)
    return jnp.concatenate(parts, axis=0)


def _conv_gate(ca_ref, cb_ref, cc_ref, w_ref, bias_ref, o_ref):
    t_len = ca_ref.shape[0]
    z = cc_ref[...].astype(F32) * ca_ref[...].astype(F32)
    edge = lax.broadcasted_iota(jnp.int32, (SUBLANES, z.shape[1]), 0)
    prev = pltpu.roll(z, 1, axis=0)
    prev = jnp.concatenate([jnp.where(edge == 0, 0.0, prev[:SUBLANES]), prev[SUBLANES:]], axis=0)
    nxt = pltpu.roll(z, t_len - 1, axis=0)
    nxt = jnp.concatenate([nxt[:-SUBLANES],
                           jnp.where(edge == SUBLANES - 1, 0.0, nxt[-SUBLANES:])], axis=0)
    zc = prev * w_ref[0:1, :] + z * w_ref[1:2, :] + nxt * w_ref[2:3, :]
    o_ref[...] = (cb_ref[...].astype(F32) * (zc + bias_ref[...])).astype(BF16)


def _hgrn_conv_body(q_ref, v_ref, zf_ref, zb_ref, g_ref, ca_ref, cb_ref, cc_ref,
                    lb_ref, gn_ref, cw_ref, cbias_ref, o_ref, ocv_ref,
                    ktf_ref, ktb_ref, qtf_ref, qtb_ref, rvf_ref, rvb_ref, s_ref, stf_ref, stb_ref,
                    gk_ref, gb_ref, rm_ref, oi_ref):
    t_len = q_ref.shape[0]
    nc = t_len // HG_C
    w = HG_DK
    mid = [HG_SB * i + HG_SB // 2 for i in range(HG_NSB)]
    ti = lax.broadcasted_iota(jnp.int32, (HG_C, HG_C), 0)
    si = lax.broadcasted_iota(jnp.int32, (HG_C, HG_C), 1)
    tri_f = jnp.where(si <= ti, 1.0, 0.0).astype(BF16)
    tri_b = jnp.where(si >= ti, 1.0, 0.0).astype(BF16)
    tr = lax.broadcasted_iota(jnp.int32, (HG_C, HG_NSB * HG_C), 0)
    cr = lax.broadcasted_iota(jnp.int32, (HG_C, HG_NSB * HG_C), 1)
    own = (cr // HG_C) == (tr // HG_SB)
    src = cr % HG_C
    rmask_f = own & (src <= tr)
    rmask_b = own & (src >= tr)
    lb_f = lb_ref[0:1, :]
    lb_b = lb_ref[1:2, :]

    _conv_gate(ca_ref, cb_ref, cc_ref, cw_ref, cbias_ref, ocv_ref)

    stf_ref[...] = jnp.zeros_like(stf_ref)
    stb_ref[...] = jnp.zeros_like(stb_ref)

    def chunk_rows(c):
        return pl.ds(pl.multiple_of(c * HG_C, HG_C), HG_C)

    n_blk = nc // HG_UNROLL
    n_cd = 2 * HG_UNROLL
    side = {False: (zf_ref, lb_f, tri_f, ktf_ref, qtf_ref, rvf_ref),
            True: (zb_ref, lb_b, tri_b, ktb_ref, qtb_ref, rvb_ref)}

    def block_work(blk):
        work = []
        for u in range(HG_UNROLL):
            n = blk * HG_UNROLL + u
            work += [(False, n), (True, nc - 1 - n)]
        return work

    def gates(blk):
        out = []
        for rev, c in block_work(blk):
            z_ref, lb, tri = side[rev][:3]
            f = lb + (1.0 - lb) * jax.nn.sigmoid(z_ref[chunk_rows(c), :].astype(F32))
            out.append((1.0 - f, _cumsum_rows(tri, jnp.log(jnp.maximum(f, F_FLOOR)))))
        return out

    def park(refs, slot, items):
        for idx, vals in enumerate(items):
            for ref, val in zip(refs, vals):
                ref[slot + idx] = val

    park((gk_ref, gb_ref), 0, gates(0))

    def states(i, slot):
        steps = []
        for idx, (rev, c) in enumerate(block_work(i)):
            k, b = gk_ref[slot + idx], gb_ref[slot + idx]
            rows = chunk_rows(c)
            refs = [b[m:m + 1, :] for m in mid]
            bend = b[0:1, :] if rev else b[HG_C - 1:HG_C, :]
            d = _per_block(refs) - b
            kt = k * jnp.exp(d)
            qt = (q_ref[rows, :].astype(F32) * jnp.exp(-d)).astype(BF16)
            kh = (kt * _per_block([jnp.exp(bend - r) for r in refs])).astype(BF16)
            upd = lax.dot_general(v_ref[rows, :], kh, _TN, preferred_element_type=F32)
            rv = jnp.concatenate(refs + refs, axis=0)
            steps.append((rev, c, kt, qt, rv, jnp.exp(bend), upd))
        return steps

    def commit_states(steps):
        st = {False: stf_ref[...], True: stb_ref[...]}
        for rev, c, kt, qt, rv, dec, upd in steps:
            kt_ref, qt_ref, rv_ref = side[rev][3:]
            rows = chunk_rows(c)
            kt_ref[rows, :] = kt
            qt_ref[rows, :] = qt
            rv_ref[c] = rv
            s_ref[c, :, (w if rev else 0):(2 * w if rev else w)] = st[rev].astype(BF16)
            st[rev] = st[rev] * dec + upd
        stf_ref[...] = st[False]
        stb_ref[...] = st[True]

    def pass1(i, carry):
        cur = (i & 1) * n_cd
        steps = states(i, cur)
        fresh = gates(i + 1)
        commit_states(steps)
        park((gk_ref, gb_ref), n_cd - cur, fresh)
        return carry

    lax.fori_loop(0, n_blk - 1, pass1, 0)
    commit_states(states(n_blk - 1, ((n_blk - 1) & 1) * n_cd))

    def scores(blk):
        pre = []
        for u in range(HG_UNROLL):
            c = blk * HG_UNROLL + u
            rows = chunk_rows(c)
            rs, qhs = [], []
            for rev in (False, True):
                kt_ref, qt_ref, rv_ref = side[rev][3:]
                rv = rv_ref[c]
                refs = [rv[j:j + 1, :] for j in range(HG_NSB)]
                qt = qt_ref[rows, :]
                kall = _key_variants(kt_ref[rows, :], refs, rev)
                rs.append(lax.dot_general(qt, kall, _NT, preferred_element_type=F32))
                qhs.append(qt.astype(F32) * _per_block([jnp.exp(r) for r in refs]))
            pre.append((c, rs, jnp.concatenate(qhs, axis=1).astype(BF16)))
        out = []
        for c, (r_f, r_b), qh in pre:
            oi = lax.dot_general(qh, s_ref[c], _NT, preferred_element_type=F32)
            rm = (jnp.where(rmask_f, r_f, 0.0) + jnp.where(rmask_b, r_b, 0.0)).astype(BF16)
            out.append((rm, oi))
        return out

    park((rm_ref, oi_ref), 0, scores(0))

    def outputs(i, slot):
        outs = []
        for u in range(HG_UNROLL):
            rows = chunk_rows(i * HG_UNROLL + u)
            v4 = jnp.concatenate([v_ref[rows, :]] * HG_NSB, axis=0)
            o = oi_ref[slot + u] + jnp.dot(rm_ref[slot + u], v4, preferred_element_type=F32)
            g = g_ref[rows, :].astype(F32)
            o = _rms_scale(o, gn_ref[...]) * (g * jax.nn.sigmoid(g))
            outs.append((rows, o.astype(BF16)))
        return outs

    def commit_outputs(outs):
        for rows, o in outs:
            o_ref[rows, :] = o

    def pass2(i, carry):
        cur = (i & 1) * HG_UNROLL
        outs = outputs(i, cur)
        fresh = scores(i + 1)
        commit_outputs(outs)
        park((rm_ref, oi_ref), HG_UNROLL - cur, fresh)
        return carry

    lax.fori_loop(0, n_blk - 1, pass2, 0)
    commit_outputs(outputs(n_blk - 1, ((n_blk - 1) & 1) * HG_UNROLL))


def _hgrn_conv(proj, lb, gnorm, conv_w, conv_b, layer, batch, t_len):
    assert (t_len // HG_C) % HG_UNROLL == 0 and MIX_W // CV_GROUP == HG_HEADS
    nc = t_len // HG_C
    spec = lambda c0: pl.BlockSpec((t_len, LANES), lambda b, h: (b, c0 + h))
    par = lambda rows: pl.BlockSpec((None, rows, LANES), lambda b, h: (layer, 0, h))
    seq = lambda dt: pltpu.VMEM((t_len, HG_DK), dt)
    rvec = pltpu.VMEM((nc, 2 * HG_NSB, HG_DK), F32)
    out = jax.ShapeDtypeStruct((batch * t_len, MIX_W), BF16)
    return pl.pallas_call(
        _hgrn_conv_body,
        grid=(batch, HG_HEADS),
        in_specs=[spec(_COL["hq"]), spec(_COL["hi"]), spec(_COL["hzf"]), spec(_COL["hzb"]),
                  spec(_COL["hg"]), spec(_COL["ca"]), spec(_COL["cb"]), spec(_COL["cc"]),
                  par(2), par(1), par(3), par(1)],
        out_specs=[pl.BlockSpec((t_len, LANES), lambda b, h: (b, h))] * 2,
        out_shape=[out, out],
        scratch_shapes=[seq(F32), seq(F32), seq(BF16), seq(BF16), rvec, rvec,
                        pltpu.VMEM((nc, HG_DK, 2 * HG_DK), BF16),
                        pltpu.VMEM((HG_DK, HG_DK), F32), pltpu.VMEM((HG_DK, HG_DK), F32),
                        pltpu.VMEM((4 * HG_UNROLL, HG_C, HG_DK), F32),
                        pltpu.VMEM((4 * HG_UNROLL, HG_C, HG_DK), F32),
                        pltpu.VMEM((2 * HG_UNROLL, HG_C, HG_NSB * HG_C), BF16),
                        pltpu.VMEM((2 * HG_UNROLL, HG_C, HG_DK), F32)],
        compiler_params=_cparams(("parallel", "parallel")),
    )(proj, proj, proj, proj, proj, proj, proj, proj, lb, gnorm, conv_w, conv_b)


def _na_bias_table(rpb):
    n_layers, n_dr = rpb.shape[0], rpb.shape[2]
    qc = np.arange(GRID_W)[:, None]
    kc = np.arange(GRID_W)[None, :]
    cs = np.clip(qc - NA_KW // 2, 0, GRID_W - NA_KW)
    col_ok = (kc >= cs) & (kc < cs + NA_KW)
    dc = np.clip(kc - qc + NA_KW - 1, 0, 2 * NA_KW - 2)
    pick = (dc[None] == np.arange(2 * NA_KW - 1)[:, None, None]) & col_ok[None]
    t = jnp.einsum("lhrc,cqk->lhrqk", rpb.astype(F32), jnp.asarray(pick, F32),
                   precision=lax.Precision.HIGHEST)
    t = t + jnp.asarray(np.where(col_ok, 0.0, MASK_VALUE), F32)
    t = t.reshape(n_layers, NA_HEADS // 2, 2, n_dr, GRID_W, GRID_W).transpose(0, 1, 3, 2, 4, 5)
    t = t.reshape(n_layers, NA_HEADS // 2, n_dr, 2 * GRID_W, GRID_W)
    return jnp.concatenate([t[:, :, :-1], t[:, :, 1:]], axis=-1)


def _na_body(q_ref, k_ref, v_ref, bias_ref, o_ref):
    n_rows = q_ref.shape[0] // GRID_W
    lane = lax.broadcasted_iota(jnp.int32, (GRID_W, LANES), 1)
    first = lane < NA_DH
    m0 = jnp.where(first, NA_SCALE, 0.0)
    m1 = jnp.where(first, 0.0, NA_SCALE)
    win = NA_KH * GRID_W

    def window(r):
        rs = jnp.clip(r - NA_KH // 2, 0, n_rows - NA_KH)
        return rs, pl.ds(pl.multiple_of(rs * GRID_W, GRID_W), win)

    def logits(r):
        rs, krows = window(r)
        q = q_ref[pl.ds(pl.multiple_of(r * GRID_W, GRID_W), GRID_W), :].astype(F32)
        q2 = jnp.concatenate([q * m0, q * m1], axis=0).astype(BF16)
        first_dr = NA_KH - 1 - (r - rs)
        bias = jnp.concatenate([bias_ref[first_dr + 2 * m] for m in range(NA_KH // 2)], axis=1)
        return lax.dot_general(q2, k_ref[krows, :], _NT, preferred_element_type=F32) + bias

    def attend(s, vw):
        p = jnp.exp(s - jnp.max(s, axis=-1, keepdims=True))
        l = jnp.sum(p, axis=-1, keepdims=True)
        o2 = jnp.dot(p.astype(BF16), vw, preferred_element_type=F32) / l
        return jnp.where(first, o2[:GRID_W, :], o2[GRID_W:, :]).astype(BF16)

    def rows_step(i, carry):
        rows = [i * NA_UNROLL + u for u in range(NA_UNROLL)]
        scores = [logits(r) for r in rows]
        outs = [attend(s, v_ref[window(r)[1], :]) for s, r in zip(scores, rows)]
        for r, o in zip(rows, outs):
            o_ref[pl.ds(pl.multiple_of(r * GRID_W, GRID_W), GRID_W), :] = o
        return carry

    lax.fori_loop(0, n_rows // NA_UNROLL, rows_step, 0)


def _natten(proj, bias, layer, batch, t_len):
    assert t_len % GRID_W == 0 and t_len // GRID_W >= NA_KH
    assert (t_len // GRID_W) % NA_UNROLL == 0
    spec = lambda c0: pl.BlockSpec((t_len, LANES), lambda p, b: (b, c0 + p))
    return pl.pallas_call(
        _na_body,
        grid=(NA_HEADS // 2, batch),
        in_specs=[spec(_COL["nq"]), spec(_COL["nk"]), spec(_COL["nv"]),
                  pl.BlockSpec((None, None, 2 * NA_KH - 2, 2 * GRID_W, 2 * GRID_W),
                               lambda p, b: (layer, p, 0, 0, 0))],
        out_specs=pl.BlockSpec((t_len, LANES), lambda p, b: (b, p)),
        out_shape=jax.ShapeDtypeStruct((batch * t_len, MIX_W), BF16),
        compiler_params=_cparams(("parallel", "parallel")),
    )(proj, proj, proj, bias)


def _mix_body(x_ref, ghg_ref, gcv_ref, gna_ref, ohg_ref, ocv_ref, ona_ref,
              whg_ref, wcv_ref, wna_ref, wout_ref, o_ref):
    ys = [jnp.dot(a_ref[...], w_ref[...], preferred_element_type=F32)
          for a_ref, w_ref in ((ohg_ref, whg_ref), (ocv_ref, wcv_ref), (ona_ref, wna_ref))]
    m = sum(jax.nn.sigmoid(g_ref[...].astype(F32)) * y
            for g_ref, y in zip((ghg_ref, gcv_ref, gna_ref), ys))
    o_ref[...] = x_ref[...] + jnp.dot(m.astype(BF16), wout_ref[...], preferred_element_type=F32)


def _mix_out(x, proj, o_hg, o_cv, o_na, w_hg, w_cv, w_na, w_out, layer):
    n, d = x.shape
    gate = lambda c0: pl.BlockSpec((TM, d), lambda i: (i, c0 * LANES // d))
    act = pl.BlockSpec((TM, MIX_W), lambda i: (i, 0))
    wmix = _resident((None, MIX_W, d), lambda i: (layer, 0, 0))
    return pl.pallas_call(
        _mix_body,
        grid=(n // TM,),
        in_specs=[pl.BlockSpec((TM, d), lambda i: (i, 0)),
                  gate(_COL["g_hg"]), gate(_COL["g_cv"]), gate(_COL["g_na"]),
                  act, act, act, wmix, wmix, wmix,
                  _resident((None, d, d), lambda i: (layer, 0, 0))],
        out_specs=pl.BlockSpec((TM, d), lambda i: (i, 0)),
        out_shape=jax.ShapeDtypeStruct((n, d), F32),
        compiler_params=_cparams(("parallel",)),
    )(x, proj, proj, proj, o_hg, o_cv, o_na, w_hg, w_cv, w_na, w_out)


def _lower_bounds(lb_logits):
    p = jax.nn.softmax(lb_logits.astype(F32), axis=1)
    return jnp.cumsum(p, axis=1) - p[:, :1]


def _trunk(x3, p):
    batch, t_len, d = x3.shape
    assert d % LANES == 0 and t_len % HG_C == 0 and (batch * t_len) % TM == 0
    n_layers = p["n_layers"]
    x = x3.reshape(batch * t_len, d)
    for l in range(n_layers):
        x = _ffn(x, p["ffn1_norm"], p["ffn1_w_gu"], p["ffn1_w_down"], l)
        proj = _inproj(x, p["mix_norm"], p["w_in"], l)
        o_hg, o_cv = _hgrn_conv(proj, p["lb"], p["hg_out_norm"], p["conv_w"], p["conv_b"],
                                l, batch, t_len)
        o_na = _natten(proj, p["na_bias"], l, batch, t_len)
        x = _mix_out(x, proj, o_hg, o_cv, o_na, p["w_hg_out"], p["w_cv_out"], p["w_na_out"],
                     p["w_out"], l)
        x = _ffn(x, p["ffn2_norm"], p["ffn2_w_gu"], p["ffn2_w_down"], l,
                 final_gain=p["final_norm"] if l == n_layers - 1 else None)
    return x.reshape(batch, t_len, d)


def kernel(x_prompt, x_sample, ffn1_norm, ffn1_w_gu, ffn1_w_down, mix_norm, w_in, hg_lb_logits, hg_out_norm, w_hg_out, conv_w, conv_b, w_cv_out, na_rpb, w_na_out, w_out, ffn2_norm, ffn2_w_gu, ffn2_w_down, final_norm):
    n_layers, d, n_in = w_in.shape
    assert n_in == N_IN
    gates0 = GATES_SRC_COL * LANES
    row = lambda a: a.astype(F32)[:, None, :]
    p = dict(
        n_layers=n_layers,
        ffn1_norm=row(ffn1_norm), ffn1_w_gu=ffn1_w_gu.astype(BF16), ffn1_w_down=ffn1_w_down.astype(BF16),
        ffn2_norm=row(ffn2_norm), ffn2_w_gu=ffn2_w_gu.astype(BF16), ffn2_w_down=ffn2_w_down.astype(BF16),
        mix_norm=row(mix_norm),
        w_in=jnp.concatenate([w_in[..., gates0:], w_in[..., :gates0]], axis=-1).astype(BF16),
        lb=_lower_bounds(hg_lb_logits).transpose(1, 0, 2),
        hg_out_norm=row(hg_out_norm),
        w_hg_out=w_hg_out.astype(BF16), w_cv_out=w_cv_out.astype(BF16), w_na_out=w_na_out.astype(BF16),
        conv_w=conv_w.astype(F32), conv_b=row(conv_b),
        na_bias=_na_bias_table(na_rpb),
        w_out=w_out.astype(BF16),
        final_norm=final_norm.astype(F32)[None, :],
    )
    return _trunk(x_prompt, p), _trunk(x_sample, p)
```

```python
import functools

import numpy as np
import jax
import jax.numpy as jnp
from jax import lax
from jax.experimental import pallas as pl
from jax.experimental.pallas import tpu as pltpu

F32 = jnp.float32
BF16 = jnp.bfloat16

GRID_W = 64
RMS_EPS = 1e-6
F_FLOOR = 1e-30
MASK_VALUE = -1e30
HG_HEADS, HG_DK = 4, 128
CV_GROUP = 128
NA_HEADS, NA_DH = 8, 64
NA_KH, NA_KW = 8, 16
NA_SCALE = NA_DH ** -0.5
MIX_W = HG_HEADS * HG_DK
GATES_SRC_COL = 44

LANES = 128
SUBLANES = 8
MXU_N = 256
VMEM_LIMIT = 56 * 1024 * 1024

TM = 512
TM_FFN = 1024
N_CHUNK = 2 * MXU_N
HG_C = 64
HG_SB = 16
HG_NSB = HG_C // HG_SB
HG_UNROLL = 8
NA_UNROLL = 8

_COL = dict(g_hg=0, g_cv=8, g_na=16, hq=24, hi=28, hzf=32, hzb=36, hg=40,
            ca=44, cb=48, cc=52, nq=56, nk=60, nv=64)
N_IN = 68 * LANES

_NT = (((1,), (1,)), ((), ()))
_TN = (((0,), (0,)), ((), ()))


def _cparams(sem):
    return pltpu.CompilerParams(dimension_semantics=sem, vmem_limit_bytes=VMEM_LIMIT)


def _resident(block_shape, index_map):
    return pl.BlockSpec(block_shape, index_map, pipeline_mode=pl.Buffered(1))


def _rms_scale(x, gain):
    return x * lax.rsqrt(jnp.mean(x * x, axis=-1, keepdims=True) + RMS_EPS) * gain


def _col_chunks(n):
    return [(c, min(c + N_CHUNK, n)) for c in range(0, n, N_CHUNK)]


def _half_swiglu(x, g_ref, wg_ref, wu_ref, wd_ref):
    xn = _rms_scale(x, g_ref[...]).astype(BF16)

    def hidden(c0, c1):
        a = jnp.dot(xn, wg_ref[:, c0:c1], preferred_element_type=F32)
        b = jnp.dot(xn, wu_ref[:, c0:c1], preferred_element_type=F32)
        return (a * jax.nn.sigmoid(a) * b).astype(BF16)

    chunks = _col_chunks(wd_ref.shape[0])
    acc = None
    h = hidden(*chunks[0])
    for idx, (c0, c1) in enumerate(chunks):
        h_next = hidden(*chunks[idx + 1]) if idx + 1 < len(chunks) else None
        part = jnp.dot(h, wd_ref[c0:c1, :], preferred_element_type=F32)
        acc = part if acc is None else acc + part
        h = h_next
    return x + 0.5 * acc


def _ffn_weight_specs(d, d_ff, layer):
    return [_resident((None, 1, d), lambda i: (layer, 0, 0)),
            _resident((None, d, d_ff), lambda i: (layer, 0, 0)),
            _resident((None, d, d_ff), lambda i: (layer, 0, 1)),
            _resident((None, d_ff, d), lambda i: (layer, 0, 0))]


def _ffn_body(x_ref, g_ref, wg_ref, wu_ref, wd_ref, o_ref):
    o_ref[...] = _half_swiglu(x_ref[...], g_ref, wg_ref, wu_ref, wd_ref)


def _ffn(x, gain, w_gu, w_down, layer):
    n, d = x.shape
    assert n % TM_FFN == 0
    return pl.pallas_call(
        _ffn_body,
        grid=(n // TM_FFN,),
        in_specs=[pl.BlockSpec((TM_FFN, d), lambda i: (i, 0))]
        + _ffn_weight_specs(d, w_down.shape[1], layer),
        out_specs=pl.BlockSpec((TM_FFN, d), lambda i: (i, 0)),
        out_shape=jax.ShapeDtypeStruct((n, d), F32),
        compiler_params=_cparams(("parallel",)),
    )(x, gain, w_gu, w_gu, w_down)


def _inproj_body(x_ref, g_ref, w_ref, o_ref):
    xn = _rms_scale(x_ref[...], g_ref[...]).astype(BF16)
    for c0, c1 in _col_chunks(w_ref.shape[1]):
        o_ref[:, c0:c1] = jnp.dot(xn, w_ref[:, c0:c1], preferred_element_type=F32).astype(BF16)


def _inproj(x, gain, w_in, layer):
    n, d = x.shape
    n_in = w_in.shape[-1]
    return pl.pallas_call(
        _inproj_body,
        grid=(n // TM,),
        in_specs=[
            pl.BlockSpec((TM, d), lambda i: (i, 0)),
            _resident((None, 1, d), lambda i: (layer, 0, 0)),
            _resident((None, d, n_in), lambda i: (layer, 0, 0)),
        ],
        out_specs=pl.BlockSpec((TM, n_in), lambda i: (i, 0)),
        out_shape=jax.ShapeDtypeStruct((n, n_in), BF16),
        compiler_params=_cparams(("parallel",)),
    )(x, gain, w_in)


def _cumsum_rows(tri, x):
    hi = x.astype(BF16)
    lo = (x - hi.astype(F32)).astype(BF16)
    s = jnp.dot(tri, jnp.concatenate([hi, lo], axis=1), preferred_element_type=F32)
    w = x.shape[1]
    return s[:, :w] + s[:, w:]


def _per_block(vecs):
    return jnp.concatenate([jnp.broadcast_to(v, (HG_SB, v.shape[1])) for v in vecs], axis=0)


def _key_variants(kt, refs, rev):
    blocks = [kt[HG_SB * j:HG_SB * (j + 1), :] for j in range(HG_NSB)]
    zero = jnp.zeros(blocks[0].shape, BF16)
    parts = []
    for i in range(HG_NSB):
        for j in range(HG_NSB):
            if j == i:
                parts.append(blocks[j].astype(BF16))
            elif (j > i) if rev else (j < i):
                parts.append((blocks[j] * jnp.exp(refs[i] - refs[j])).astype(BF16))
            else:
                parts.append(zero)
    return jnp.concatenate(parts, axis=0)


def _conv_gate(ca_ref, cb_ref, cc_ref, w_ref, bias_ref, o_ref):
    t_len = ca_ref.shape[0]
    z = cc_ref[...].astype(F32) * ca_ref[...].astype(F32)
    edge = lax.broadcasted_iota(jnp.int32, (SUBLANES, z.shape[1]), 0)
    prev = pltpu.roll(z, 1, axis=0)
    prev = jnp.concatenate([jnp.where(edge == 0, 0.0, prev[:SUBLANES]), prev[SUBLANES:]], axis=0)
    nxt = pltpu.roll(z, t_len - 1, axis=0)
    nxt = jnp.concatenate([nxt[:-SUBLANES],
                           jnp.where(edge == SUBLANES - 1, 0.0, nxt[-SUBLANES:])], axis=0)
    zc = prev * w_ref[0:1, :] + z * w_ref[1:2, :] + nxt * w_ref[2:3, :]
    o_ref[...] = (cb_ref[...].astype(F32) * (zc + bias_ref[...])).astype(BF16)


def _hgrn_conv_body(q_ref, v_ref, zf_ref, zb_ref, g_ref, ca_ref, cb_ref, cc_ref,
                    lb_ref, gn_ref, cw_ref, cbias_ref, o_ref, ocv_ref,
                    ktf_ref, ktb_ref, qtf_ref, qtb_ref, rvf_ref, rvb_ref, s_ref, stf_ref, stb_ref,
                    gk_ref, gb_ref, rm_ref, oi_ref):
    t_len = q_ref.shape[0]
    nc = t_len // HG_C
    w = HG_DK
    mid = [HG_SB * i + HG_SB // 2 for i in range(HG_NSB)]
    ti = lax.broadcasted_iota(jnp.int32, (HG_C, HG_C), 0)
    si = lax.broadcasted_iota(jnp.int32, (HG_C, HG_C), 1)
    tri_f = jnp.where(si <= ti, 1.0, 0.0).astype(BF16)
    tri_b = jnp.where(si >= ti, 1.0, 0.0).astype(BF16)
    tr = lax.broadcasted_iota(jnp.int32, (HG_C, HG_NSB * HG_C), 0)
    cr = lax.broadcasted_iota(jnp.int32, (HG_C, HG_NSB * HG_C), 1)
    own = (cr // HG_C) == (tr // HG_SB)
    src = cr % HG_C
    rmask_f = own & (src <= tr)
    rmask_b = own & (src >= tr)
    lb_f = lb_ref[0:1, :]
    lb_b = lb_ref[1:2, :]

    _conv_gate(ca_ref, cb_ref, cc_ref, cw_ref, cbias_ref, ocv_ref)

    stf_ref[...] = jnp.zeros_like(stf_ref)
    stb_ref[...] = jnp.zeros_like(stb_ref)

    def chunk_rows(c):
        return pl.ds(pl.multiple_of(c * HG_C, HG_C), HG_C)

    n_blk = nc // HG_UNROLL
    n_cd = 2 * HG_UNROLL
    side = {False: (zf_ref, lb_f, tri_f, ktf_ref, qtf_ref, rvf_ref),
            True: (zb_ref, lb_b, tri_b, ktb_ref, qtb_ref, rvb_ref)}

    def block_work(blk):
        work = []
        for u in range(HG_UNROLL):
            n = blk * HG_UNROLL + u
            work += [(False, n), (True, nc - 1 - n)]
        return work

    def gates(blk):
        out = []
        for rev, c in block_work(blk):
            z_ref, lb, tri = side[rev][:3]
            f = lb + (1.0 - lb) * jax.nn.sigmoid(z_ref[chunk_rows(c), :].astype(F32))
            out.append((1.0 - f, _cumsum_rows(tri, jnp.log(jnp.maximum(f, F_FLOOR)))))
        return out

    def park(refs, slot, items):
        for idx, vals in enumerate(items):
            for ref, val in zip(refs, vals):
                ref[slot + idx] = val

    park((gk_ref, gb_ref), 0, gates(0))

    def states(i, slot):
        steps = []
        for idx, (rev, c) in enumerate(block_work(i)):
            k, b = gk_ref[slot + idx], gb_ref[slot + idx]
            rows = chunk_rows(c)
            refs = [b[m:m + 1, :] for m in mid]
            bend = b[0:1, :] if rev else b[HG_C - 1:HG_C, :]
            d = _per_block(refs) - b
            kt = k * jnp.exp(d)
            qt = (q_ref[rows, :].astype(F32) * jnp.exp(-d)).astype(BF16)
            kh = (kt * _per_block([jnp.exp(bend - r) for r in refs])).astype(BF16)
            upd = lax.dot_general(v_ref[rows, :], kh, _TN, preferred_element_type=F32)
            rv = jnp.concatenate(refs + refs, axis=0)
            steps.append((rev, c, kt, qt, rv, jnp.exp(bend), upd))
        return steps

    def commit_states(steps):
        st = {False: stf_ref[...], True: stb_ref[...]}
        for rev, c, kt, qt, rv, dec, upd in steps:
            kt_ref, qt_ref, rv_ref = side[rev][3:]
            rows = chunk_rows(c)
            kt_ref[rows, :] = kt
            qt_ref[rows, :] = qt
            rv_ref[c] = rv
            s_ref[c, :, (w if rev else 0):(2 * w if rev else w)] = st[rev].astype(BF16)
            st[rev] = st[rev] * dec + upd
        stf_ref[...] = st[False]
        stb_ref[...] = st[True]

    def pass1(i, carry):
        cur = (i & 1) * n_cd
        steps = states(i, cur)
        fresh = gates(i + 1)
        commit_states(steps)
        park((gk_ref, gb_ref), n_cd - cur, fresh)
        return carry

    lax.fori_loop(0, n_blk - 1, pass1, 0)
    commit_states(states(n_blk - 1, ((n_blk - 1) & 1) * n_cd))

    def scores(blk):
        pre = []
        for u in range(HG_UNROLL):
            c = blk * HG_UNROLL + u
            rows = chunk_rows(c)
            rs, qhs = [], []
            for rev in (False, True):
                kt_ref, qt_ref, rv_ref = side[rev][3:]
                rv = rv_ref[c]
                refs = [rv[j:j + 1, :] for j in range(HG_NSB)]
                qt = qt_ref[rows, :]
                kall = _key_variants(kt_ref[rows, :], refs, rev)
                rs.append(lax.dot_general(qt, kall, _NT, preferred_element_type=F32))
                qhs.append(qt.astype(F32) * _per_block([jnp.exp(r) for r in refs]))
            pre.append((c, rs, jnp.concatenate(qhs, axis=1).astype(BF16)))
        out = []
        for c, (r_f, r_b), qh in pre:
            oi = lax.dot_general(qh, s_ref[c], _NT, preferred_element_type=F32)
            rm = (jnp.where(rmask_f, r_f, 0.0) + jnp.where(rmask_b, r_b, 0.0)).astype(BF16)
            out.append((rm, oi))
        return out

    park((rm_ref, oi_ref), 0, scores(0))

    def outputs(i, slot):
        outs = []
        for u in range(HG_UNROLL):
            rows = chunk_rows(i * HG_UNROLL + u)
            v4 = jnp.concatenate([v_ref[rows, :]] * HG_NSB, axis=0)
            o = oi_ref[slot + u] + jnp.dot(rm_ref[slot + u], v4, preferred_element_type=F32)
            g = g_ref[rows, :].astype(F32)
            o = _rms_scale(o, gn_ref[...]) * (g * jax.nn.sigmoid(g))
            outs.append((rows, o.astype(BF16)))
        return outs

    def commit_outputs(outs):
        for rows, o in outs:
            o_ref[rows, :] = o

    def pass2(i, carry):
        cur = (i & 1) * HG_UNROLL
        outs = outputs(i, cur)
        fresh = scores(i + 1)
        commit_outputs(outs)
        park((rm_ref, oi_ref), HG_UNROLL - cur, fresh)
        return carry

    lax.fori_loop(0, n_blk - 1, pass2, 0)
    commit_outputs(outputs(n_blk - 1, ((n_blk - 1) & 1) * HG_UNROLL))


def _hgrn_conv(proj, lb, gnorm, conv_w, conv_b, layer, batch, t_len):
    assert (t_len // HG_C) % HG_UNROLL == 0 and MIX_W // CV_GROUP == HG_HEADS
    nc = t_len // HG_C
    spec = lambda c0: pl.BlockSpec((t_len, LANES), lambda b, h: (b, c0 + h))
    par = lambda rows: pl.BlockSpec((None, rows, LANES), lambda b, h: (layer, 0, h))
    seq = lambda dt: pltpu.VMEM((t_len, HG_DK), dt)
    rvec = pltpu.VMEM((nc, 2 * HG_NSB, HG_DK), F32)
    out = jax.ShapeDtypeStruct((batch * t_len, MIX_W), BF16)
    return pl.pallas_call(
        _hgrn_conv_body,
        grid=(batch, HG_HEADS),
        in_specs=[spec(_COL["hq"]), spec(_COL["hi"]), spec(_COL["hzf"]), spec(_COL["hzb"]),
                  spec(_COL["hg"]), spec(_COL["ca"]), spec(_COL["cb"]), spec(_COL["cc"]),
                  par(2), par(1), par(3), par(1)],
        out_specs=[pl.BlockSpec((t_len, LANES), lambda b, h: (b, h))] * 2,
        out_shape=[out, out],
        scratch_shapes=[seq(F32), seq(F32), seq(BF16), seq(BF16), rvec, rvec,
                        pltpu.VMEM((nc, HG_DK, 2 * HG_DK), BF16),
                        pltpu.VMEM((HG_DK, HG_DK), F32), pltpu.VMEM((HG_DK, HG_DK), F32),
                        pltpu.VMEM((4 * HG_UNROLL, HG_C, HG_DK), F32),
                        pltpu.VMEM((4 * HG_UNROLL, HG_C, HG_DK), F32),
                        pltpu.VMEM((2 * HG_UNROLL, HG_C, HG_NSB * HG_C), BF16),
                        pltpu.VMEM((2 * HG_UNROLL, HG_C, HG_DK), F32)],
        compiler_params=_cparams(("parallel", "parallel")),
    )(proj, proj, proj, proj, proj, proj, proj, proj, lb, gnorm, conv_w, conv_b)


def _na_bias_table(rpb):
    n_layers, n_dr = rpb.shape[0], rpb.shape[2]
    qc = np.arange(GRID_W)[:, None]
    kc = np.arange(GRID_W)[None, :]
    cs = np.clip(qc - NA_KW // 2, 0, GRID_W - NA_KW)
    col_ok = (kc >= cs) & (kc < cs + NA_KW)
    dc = np.clip(kc - qc + NA_KW - 1, 0, 2 * NA_KW - 2)
    pick = (dc[None] == np.arange(2 * NA_KW - 1)[:, None, None]) & col_ok[None]
    t = jnp.einsum("lhrc,cqk->lhrqk", rpb.astype(F32), jnp.asarray(pick, F32),
                   precision=lax.Precision.HIGHEST)
    t = t + jnp.asarray(np.where(col_ok, 0.0, MASK_VALUE), F32)
    t = t.reshape(n_layers, NA_HEADS // 2, 2, n_dr, GRID_W, GRID_W).transpose(0, 1, 3, 2, 4, 5)
    t = t.reshape(n_layers, NA_HEADS // 2, n_dr, 2 * GRID_W, GRID_W)
    return jnp.concatenate([t[:, :, :-1], t[:, :, 1:]], axis=-1)


def _na_body(q_ref, k_ref, v_ref, bias_ref, o_ref):
    n_rows = q_ref.shape[0] // GRID_W
    lane = lax.broadcasted_iota(jnp.int32, (GRID_W, LANES), 1)
    first = lane < NA_DH
    m0 = jnp.where(first, NA_SCALE, 0.0)
    m1 = jnp.where(first, 0.0, NA_SCALE)
    win = NA_KH * GRID_W

    def window(r):
        rs = jnp.clip(r - NA_KH // 2, 0, n_rows - NA_KH)
        return rs, pl.ds(pl.multiple_of(rs * GRID_W, GRID_W), win)

    def logits(r):
        rs, krows = window(r)
        q = q_ref[pl.ds(pl.multiple_of(r * GRID_W, GRID_W), GRID_W), :].astype(F32)
        q2 = jnp.concatenate([q * m0, q * m1], axis=0).astype(BF16)
        first_dr = NA_KH - 1 - (r - rs)
        bias = jnp.concatenate([bias_ref[first_dr + 2 * m] for m in range(NA_KH // 2)], axis=1)
        return lax.dot_general(q2, k_ref[krows, :], _NT, preferred_element_type=F32) + bias

    def attend(s, vw):
        p = jnp.exp(s - jnp.max(s, axis=-1, keepdims=True))
        l = jnp.sum(p, axis=-1, keepdims=True)
        o2 = jnp.dot(p.astype(BF16), vw, preferred_element_type=F32) / l
        return jnp.where(first, o2[:GRID_W, :], o2[GRID_W:, :]).astype(BF16)

    def rows_step(i, carry):
        rows = [i * NA_UNROLL + u for u in range(NA_UNROLL)]
        scores = [logits(r) for r in rows]
        outs = [attend(s, v_ref[window(r)[1], :]) for s, r in zip(scores, rows)]
        for r, o in zip(rows, outs):
            o_ref[pl.ds(pl.multiple_of(r * GRID_W, GRID_W), GRID_W), :] = o
        return carry

    lax.fori_loop(0, n_rows // NA_UNROLL, rows_step, 0)


def _natten(proj, bias, layer, batch, t_len):
    assert t_len % GRID_W == 0 and t_len // GRID_W >= NA_KH
    assert (t_len // GRID_W) % NA_UNROLL == 0
    spec = lambda c0: pl.BlockSpec((t_len, LANES), lambda p, b: (b, c0 + p))
    return pl.pallas_call(
        _na_body,
        grid=(NA_HEADS // 2, batch),
        in_specs=[spec(_COL["nq"]), spec(_COL["nk"]), spec(_COL["nv"]),
                  pl.BlockSpec((None, None, 2 * NA_KH - 2, 2 * GRID_W, 2 * GRID_W),
                               lambda p, b: (layer, p, 0, 0, 0))],
        out_specs=pl.BlockSpec((t_len, LANES), lambda p, b: (b, p)),
        out_shape=jax.ShapeDtypeStruct((batch * t_len, MIX_W), BF16),
        compiler_params=_cparams(("parallel", "parallel")),
    )(proj, proj, proj, bias)


def _mix_ffn_body(x_ref, ghg_ref, gcv_ref, gna_ref, ohg_ref, ocv_ref, ona_ref,
                  whg_ref, wcv_ref, wna_ref, wout_ref, g_ref, wg_ref, wu_ref, wd_ref, *rest, final):
    o_ref = rest[-1]
    ys = [jnp.dot(a_ref[...], w_ref[...], preferred_element_type=F32)
          for a_ref, w_ref in ((ohg_ref, whg_ref), (ocv_ref, wcv_ref), (ona_ref, wna_ref))]
    m = sum(jax.nn.sigmoid(g_ref[...].astype(F32)) * y
            for g_ref, y in zip((ghg_ref, gcv_ref, gna_ref), ys))
    x = x_ref[...] + jnp.dot(m.astype(BF16), wout_ref[...], preferred_element_type=F32)
    y = _half_swiglu(x, g_ref, wg_ref, wu_ref, wd_ref)
    if final:
        y = _rms_scale(y, rest[0][...])
    o_ref[...] = y


def _mix_ffn(x, proj, o_hg, o_cv, o_na, w_hg, w_cv, w_na, w_out, gain, w_gu, w_down, layer,
             final_gain=None):
    n, d = x.shape
    final = final_gain is not None
    gate = lambda c0: pl.BlockSpec((TM, d), lambda i: (i, c0 * LANES // d))
    act = pl.BlockSpec((TM, MIX_W), lambda i: (i, 0))
    wmix = _resident((None, MIX_W, d), lambda i: (layer, 0, 0))
    in_specs = [pl.BlockSpec((TM, d), lambda i: (i, 0)),
                gate(_COL["g_hg"]), gate(_COL["g_cv"]), gate(_COL["g_na"]),
                act, act, act, wmix, wmix, wmix,
                _resident((None, d, d), lambda i: (layer, 0, 0))]
    in_specs += _ffn_weight_specs(d, w_down.shape[1], layer)
    args = [x, proj, proj, proj, o_hg, o_cv, o_na, w_hg, w_cv, w_na, w_out,
            gain, w_gu, w_gu, w_down]
    if final:
        in_specs.append(_resident((1, d), lambda i: (0, 0)))
        args.append(final_gain)
    return pl.pallas_call(
        functools.partial(_mix_ffn_body, final=final),
        grid=(n // TM,),
        in_specs=in_specs,
        out_specs=pl.BlockSpec((TM, d), lambda i: (i, 0)),
        out_shape=jax.ShapeDtypeStruct((n, d), F32),
        compiler_params=_cparams(("parallel",)),
    )(*args)


def _lower_bounds(lb_logits):
    p = jax.nn.softmax(lb_logits.astype(F32), axis=1)
    return jnp.cumsum(p, axis=1) - p[:, :1]


def _trunk(x3, p):
    batch, t_len, d = x3.shape
    assert d % LANES == 0 and t_len % HG_C == 0 and (batch * t_len) % TM == 0
    n_layers = p["n_layers"]
    x = x3.reshape(batch * t_len, d)
    for l in range(n_layers):
        x = _ffn(x, p["ffn1_norm"], p["ffn1_w_gu"], p["ffn1_w_down"], l)
        proj = _inproj(x, p["mix_norm"], p["w_in"], l)
        o_hg, o_cv = _hgrn_conv(proj, p["lb"], p["hg_out_norm"], p["conv_w"], p["conv_b"],
                                l, batch, t_len)
        o_na = _natten(proj, p["na_bias"], l, batch, t_len)
        x = _mix_ffn(x, proj, o_hg, o_cv, o_na, p["w_hg_out"], p["w_cv_out"], p["w_na_out"],
                     p["w_out"], p["ffn2_norm"], p["ffn2_w_gu"], p["ffn2_w_down"], l,
                     final_gain=p["final_norm"] if l == n_layers - 1 else None)
    return x.reshape(batch, t_len, d)


def kernel(x_prompt, x_sample, ffn1_norm, ffn1_w_gu, ffn1_w_down, mix_norm, w_in, hg_lb_logits, hg_out_norm, w_hg_out, conv_w, conv_b, w_cv_out, na_rpb, w_na_out, w_out, ffn2_norm, ffn2_w_gu, ffn2_w_down, final_norm):
    n_layers, d, n_in = w_in.shape
    assert n_in == N_IN
    gates0 = GATES_SRC_COL * LANES
    row = lambda a: a.astype(F32)[:, None, :]
    p = dict(
        n_layers=n_layers,
        ffn1_norm=row(ffn1_norm), ffn1_w_gu=ffn1_w_gu.astype(BF16), ffn1_w_down=ffn1_w_down.astype(BF16),
        ffn2_norm=row(ffn2_norm), ffn2_w_gu=ffn2_w_gu.astype(BF16), ffn2_w_down=ffn2_w_down.astype(BF16),
        mix_norm=row(mix_norm),
        w_in=jnp.concatenate([w_in[..., gates0:], w_in[..., :gates0]], axis=-1).astype(BF16),
        lb=_lower_bounds(hg_lb_logits).transpose(1, 0, 2),
        hg_out_norm=row(hg_out_norm),
        w_hg_out=w_hg_out.astype(BF16), w_cv_out=w_cv_out.astype(BF16), w_na_out=w_na_out.astype(BF16),
        conv_w=conv_w.astype(F32), conv_b=row(conv_b),
        na_bias=_na_bias_table(na_rpb),
        w_out=w_out.astype(BF16),
        final_norm=final_norm.astype(F32)[None, :],
    )
    return _trunk(x_prompt, p), _trunk(x_sample, p)
```

```python
import functools

import numpy as np
import jax
import jax.numpy as jnp
from jax import lax
from jax.experimental import pallas as pl
from jax.experimental.pallas import tpu as pltpu

F32 = jnp.float32
BF16 = jnp.bfloat16

GRID_W = 64
RMS_EPS = 1e-6
F_FLOOR = 1e-30
MASK_VALUE = -1e30
HG_HEADS, HG_DK = 4, 128
CV_GROUP = 128
NA_HEADS, NA_DH = 8, 64
NA_KH, NA_KW = 8, 16
NA_SCALE = NA_DH ** -0.5
MIX_W = HG_HEADS * HG_DK

LANES = 128
SUBLANES = 8
MXU_N = 256
VMEM_LIMIT = 56 * 1024 * 1024

TM = 512
TM_FFN = 1024
N_CHUNK = 2 * MXU_N
HG_C = 64
HG_SB = 16
HG_NSB = HG_C // HG_SB
HG_UNROLL = 8
NA_UNROLL = 8

_SRC = dict(hq=(0, 4), hi=(4, 8), hzf=(8, 12), hzb=(12, 16), hg=(16, 20), ca=(20, 24), cb=(24, 28),
            cc=(28, 32), nq=(32, 36), nk=(36, 40), nv=(40, 44), g_hg=(44, 52), g_cv=(52, 60),
            g_na=(60, 68))
_ORDER = ("g_hg", "g_cv", "g_na", "hi", "hg", "ca", "cb", "cc", "nq", "nk", "nv", "hq", "hzf", "hzb")
_COL = {}
for _name in _ORDER:
    _COL[_name] = sum(_SRC[m][1] - _SRC[m][0] for m in _ORDER[:_ORDER.index(_name)])
N_IN = 68 * LANES
N_STORE = _COL["hq"] * LANES

_NT = (((1,), (1,)), ((), ()))
_TN = (((0,), (0,)), ((), ()))


def _cparams(sem):
    return pltpu.CompilerParams(dimension_semantics=sem, vmem_limit_bytes=VMEM_LIMIT)


def _resident(block_shape, index_map):
    return pl.BlockSpec(block_shape, index_map, pipeline_mode=pl.Buffered(1))


def _rms_scale(x, gain):
    return x * lax.rsqrt(jnp.mean(x * x, axis=-1, keepdims=True) + RMS_EPS) * gain


def _col_chunks(n):
    return [(c, min(c + N_CHUNK, n)) for c in range(0, n, N_CHUNK)]


def _half_swiglu(x, g_ref, wg_ref, wu_ref, wd_ref):
    xn = _rms_scale(x, g_ref[...]).astype(BF16)

    def hidden(c0, c1):
        a = jnp.dot(xn, wg_ref[:, c0:c1], preferred_element_type=F32)
        b = jnp.dot(xn, wu_ref[:, c0:c1], preferred_element_type=F32)
        return (a * jax.nn.sigmoid(a) * b).astype(BF16)

    chunks = _col_chunks(wd_ref.shape[0])
    acc = None
    h = hidden(*chunks[0])
    for idx, (c0, c1) in enumerate(chunks):
        h_next = hidden(*chunks[idx + 1]) if idx + 1 < len(chunks) else None
        part = jnp.dot(h, wd_ref[c0:c1, :], preferred_element_type=F32)
        acc = part if acc is None else acc + part
        h = h_next
    return x + 0.5 * acc


def _ffn_weight_specs(d, d_ff, layer):
    return [_resident((None, 1, d), lambda i: (layer, 0, 0)),
            _resident((None, d, d_ff), lambda i: (layer, 0, 0)),
            _resident((None, d, d_ff), lambda i: (layer, 0, 1)),
            _resident((None, d_ff, d), lambda i: (layer, 0, 0))]


def _ffn_body(x_ref, g_ref, wg_ref, wu_ref, wd_ref, o_ref):
    o_ref[...] = _half_swiglu(x_ref[...], g_ref, wg_ref, wu_ref, wd_ref)


def _ffn(x, gain, w_gu, w_down, layer):
    n, d = x.shape
    assert n % TM_FFN == 0
    return pl.pallas_call(
        _ffn_body,
        grid=(n // TM_FFN,),
        in_specs=[pl.BlockSpec((TM_FFN, d), lambda i: (i, 0))]
        + _ffn_weight_specs(d, w_down.shape[1], layer),
        out_specs=pl.BlockSpec((TM_FFN, d), lambda i: (i, 0)),
        out_shape=jax.ShapeDtypeStruct((n, d), F32),
        compiler_params=_cparams(("parallel",)),
    )(x, gain, w_gu, w_gu, w_down)


def _running_sum_matrix():
    tok = np.arange(TM)
    chunk, pos = tok // HG_C, tok % HG_C
    ref_tok = np.arange(TM // HG_SB) * HG_SB + HG_SB // 2
    mats = []
    for rev in (False, True):
        end_tok = np.arange(TM // HG_C) * HG_C + (0 if rev else HG_C - 1)
        at = np.concatenate([tok, ref_tok, end_tok])
        same = chunk[None, :] == chunk[at][:, None]
        upto = pos[None, :] >= pos[at][:, None] if rev else pos[None, :] <= pos[at][:, None]
        pad = np.zeros((-len(at) % (2 * SUBLANES), TM), bool)
        mats.append(np.concatenate([same & upto, pad], axis=0))
    return jnp.asarray(np.stack(mats).astype(np.float32), BF16)


def _inproj_body(x_ref, g_ref, w_ref, lb_ref, sum_ref,
                 o_ref, qt_ref, kt_ref, kh_ref, rv_ref, be_ref):
    xn = _rms_scale(x_ref[...], g_ref[...]).astype(BF16)
    n_store = o_ref.shape[1]
    w = kt_ref.shape[1] // 2

    def proj(c0, c1):
        return jnp.dot(xn, w_ref[:, c0:c1], preferred_element_type=F32)

    q = proj(n_store, n_store + w)
    gated = []
    for side in range(2):
        z = proj(n_store + (1 + side) * w, n_store + (2 + side) * w)
        lb = lb_ref[side:side + 1, :]
        f = lb + (1.0 - lb) * jax.nn.sigmoid(z)
        gated.append((1.0 - f, jnp.log(jnp.maximum(f, F_FLOOR)).astype(BF16)))
    pieces = [(side, c0) for side in range(2) for c0 in range(0, w, MXU_N)]
    slab = _col_chunks(n_store)
    every = len(slab) // (len(pieces) + 1)
    sums = {}
    for idx, (c0, c1) in enumerate(slab):
        o_ref[:, c0:c1] = proj(c0, c1).astype(BF16)
        if (idx + 1) % every == 0 and len(sums) < len(pieces):
            side, p0 = pieces[len(sums)]
            sums[side, p0] = jnp.dot(sum_ref[side], gated[side][1][:, p0:p0 + MXU_N],
                                     preferred_element_type=F32)
    n_r, n_e = TM // HG_SB, TM // HG_C
    for (side, p0), full in sums.items():
        k = gated[side][0][:, p0:p0 + MXU_N]
        b = full[:TM]
        r_rows, e_rows = [], []
        for r0 in range(0, TM, HG_C):
            e = b[r0:r0 + 1, :] if side else b[r0 + HG_C - 1:r0 + HG_C, :]
            e_rows.append(jnp.broadcast_to(e, (HG_C, MXU_N)))
            for s0 in range(r0, r0 + HG_C, HG_SB):
                r = b[s0 + HG_SB // 2:s0 + HG_SB // 2 + 1, :]
                r_rows.append(jnp.broadcast_to(r, (HG_SB, MXU_N)))
        r_rows = jnp.concatenate(r_rows, axis=0)
        e_rows = jnp.concatenate(e_rows, axis=0)
        d = r_rows - b
        cols = slice(side * w + p0, side * w + p0 + MXU_N)
        kt_ref[:, cols] = (k * jnp.exp(d)).astype(BF16)
        qt_ref[:, cols] = (q[:, p0:p0 + MXU_N] * jnp.exp(-d)).astype(BF16)
        kh_ref[:, cols] = (k * jnp.exp(e_rows - b)).astype(BF16)
        rv_ref[:, cols] = full[TM:TM + n_r]
        be_ref[:, cols] = full[TM + n_r:TM + n_r + n_e]


def _inproj(x, gain, w_in, lb, sums, layer):
    n, d = x.shape
    n_in = w_in.shape[-1]
    w2 = 2 * MIX_W
    rows = lambda r: pl.BlockSpec((r, w2), lambda i: (i, 0))
    return pl.pallas_call(
        _inproj_body,
        grid=(n // TM,),
        in_specs=[
            pl.BlockSpec((TM, d), lambda i: (i, 0)),
            _resident((None, 1, d), lambda i: (layer, 0, 0)),
            _resident((None, d, n_in), lambda i: (layer, 0, 0)),
            _resident((None, 2, MIX_W), lambda i: (layer, 0, 0)),
            _resident(sums.shape, lambda i: (0, 0, 0)),
        ],
        out_specs=[pl.BlockSpec((TM, N_STORE), lambda i: (i, 0)), rows(TM), rows(TM), rows(TM),
                   rows(TM // HG_SB), rows(TM // HG_C)],
        out_shape=[jax.ShapeDtypeStruct((n, N_STORE), BF16)]
        + [jax.ShapeDtypeStruct((n, w2), BF16)] * 3
        + [jax.ShapeDtypeStruct((n // HG_SB, w2), F32),
           jax.ShapeDtypeStruct((n // HG_C, w2), F32)],
        compiler_params=_cparams(("parallel",)),
    )(x, gain, w_in, lb, sums)


def _per_block(vecs):
    return jnp.concatenate([jnp.broadcast_to(v, (HG_SB, v.shape[1])) for v in vecs], axis=0)


def _key_variants(kt, refs, rev):
    blocks = [kt[HG_SB * j:HG_SB * (j + 1), :] for j in range(HG_NSB)]
    wide = [blk.astype(F32) for blk in blocks]
    zero = jnp.zeros(blocks[0].shape, BF16)
    parts = []
    for i in range(HG_NSB):
        for j in range(HG_NSB):
            if j == i:
                parts.append(blocks[j])
            elif (j > i) if rev else (j < i):
                parts.append((wide[j] * jnp.exp(refs[i] - refs[j])).astype(BF16))
            else:
                parts.append(zero)
    return jnp.concatenate(parts, axis=0)


def _conv_gate(ca_ref, cb_ref, cc_ref, w_ref, bias_ref, o_ref):
    t_len = ca_ref.shape[0]
    z = cc_ref[...].astype(F32) * ca_ref[...].astype(F32)
    edge = lax.broadcasted_iota(jnp.int32, (SUBLANES, z.shape[1]), 0)
    prev = pltpu.roll(z, 1, axis=0)
    prev = jnp.concatenate([jnp.where(edge == 0, 0.0, prev[:SUBLANES]), prev[SUBLANES:]], axis=0)
    nxt = pltpu.roll(z, t_len - 1, axis=0)
    nxt = jnp.concatenate([nxt[:-SUBLANES],
                           jnp.where(edge == SUBLANES - 1, 0.0, nxt[-SUBLANES:])], axis=0)
    zc = prev * w_ref[0:1, :] + z * w_ref[1:2, :] + nxt * w_ref[2:3, :]
    o_ref[...] = (cb_ref[...].astype(F32) * (zc + bias_ref[...])).astype(BF16)


def _hgrn_conv_body(v_ref, g_ref, ca_ref, cb_ref, cc_ref,
                    qtf_ref, qtb_ref, ktf_ref, ktb_ref, khf_ref, khb_ref,
                    rvf_ref, rvb_ref, bef_ref, beb_ref, gn_ref, cw_ref, cbias_ref,
                    o_ref, ocv_ref, s_ref, stf_ref, stb_ref, rm_ref, oi_ref):
    t_len = v_ref.shape[0]
    nc = t_len // HG_C
    w = HG_DK
    tr = lax.broadcasted_iota(jnp.int32, (HG_C, HG_NSB * HG_C), 0)
    cr = lax.broadcasted_iota(jnp.int32, (HG_C, HG_NSB * HG_C), 1)
    own = (cr // HG_C) == (tr // HG_SB)
    src = cr % HG_C
    rmask_f = own & (src <= tr)
    rmask_b = own & (src >= tr)
    side = {False: (qtf_ref, ktf_ref, khf_ref, rvf_ref, bef_ref),
            True: (qtb_ref, ktb_ref, khb_ref, rvb_ref, beb_ref)}

    _conv_gate(ca_ref, cb_ref, cc_ref, cw_ref, cbias_ref, ocv_ref)

    def chunk_rows(c):
        return pl.ds(pl.multiple_of(c * HG_C, HG_C), HG_C)

    def park(refs, slot, items):
        for idx, vals in enumerate(items):
            for ref, val in zip(refs, vals):
                ref[slot + idx] = val

    n_blk = nc // HG_UNROLL

    def pass1(i, carry):
        first = i * HG_UNROLL
        last = nc - HG_UNROLL - first
        decs = {False: jnp.exp(bef_ref[pl.ds(pl.multiple_of(first, HG_UNROLL), HG_UNROLL), :]),
                True: jnp.exp(beb_ref[pl.ds(pl.multiple_of(last, HG_UNROLL), HG_UNROLL), :])}
        steps = []
        for u in range(HG_UNROLL):
            for rev, c, row in ((False, first + u, u), (True, nc - 1 - first - u, HG_UNROLL - 1 - u)):
                upd = lax.dot_general(v_ref[chunk_rows(c), :], side[rev][2][chunk_rows(c), :], _TN,
                                      preferred_element_type=F32)
                steps.append((rev, c, decs[rev][row:row + 1, :], upd))
        st = {False: stf_ref[...], True: stb_ref[...]}
        for rev, c, dec, upd in steps:
            s_ref[c, :, (w if rev else 0):(2 * w if rev else w)] = st[rev].astype(BF16)
            st[rev] = st[rev] * dec + upd
        stf_ref[...] = st[False]
        stb_ref[...] = st[True]
        return carry

    stf_ref[...] = jnp.zeros_like(stf_ref)
    stb_ref[...] = jnp.zeros_like(stb_ref)
    lax.fori_loop(0, n_blk, pass1, 0)

    def scores(blk):
        pre = []
        blk_refs = pl.ds(pl.multiple_of(blk * (HG_UNROLL * HG_NSB), HG_UNROLL * HG_NSB),
                         HG_UNROLL * HG_NSB)
        rvs = {rev: side[rev][3][blk_refs, :] for rev in (False, True)}
        for u in range(HG_UNROLL):
            c = blk * HG_UNROLL + u
            rows = chunk_rows(c)
            rs, qhs = [], []
            for rev in (False, True):
                qt_ref, kt_ref = side[rev][:2]
                refs = [rvs[rev][HG_NSB * u + j:HG_NSB * u + j + 1, :] for j in range(HG_NSB)]
                qt = qt_ref[rows, :]
                kall = _key_variants(kt_ref[rows, :], refs, rev)
                rs.append(lax.dot_general(qt, kall, _NT, preferred_element_type=F32))
                qhs.append(qt.astype(F32) * _per_block([jnp.exp(r) for r in refs]))
            pre.append((c, rs, jnp.concatenate(qhs, axis=1).astype(BF16)))
        out = []
        for c, (r_f, r_b), qh in pre:
            oi = lax.dot_general(qh, s_ref[c], _NT, preferred_element_type=F32)
            rm = (jnp.where(rmask_f, r_f, 0.0) + jnp.where(rmask_b, r_b, 0.0)).astype(BF16)
            out.append((rm, oi))
        return out

    park((rm_ref, oi_ref), 0, scores(0))

    def outputs(i, slot):
        outs = []
        for u in range(HG_UNROLL):
            rows = chunk_rows(i * HG_UNROLL + u)
            v4 = jnp.concatenate([v_ref[rows, :]] * HG_NSB, axis=0)
            o = oi_ref[slot + u] + jnp.dot(rm_ref[slot + u], v4, preferred_element_type=F32)
            g = g_ref[rows, :].astype(F32)
            o = _rms_scale(o, gn_ref[...]) * (g * jax.nn.sigmoid(g))
            outs.append((rows, o.astype(BF16)))
        return outs

    def commit_outputs(outs):
        for rows, o in outs:
            o_ref[rows, :] = o

    def pass2(i, carry):
        cur = (i & 1) * HG_UNROLL
        outs = outputs(i, cur)
        fresh = scores(i + 1)
        commit_outputs(outs)
        park((rm_ref, oi_ref), HG_UNROLL - cur, fresh)
        return carry

    lax.fori_loop(0, n_blk - 1, pass2, 0)
    commit_outputs(outputs(n_blk - 1, ((n_blk - 1) & 1) * HG_UNROLL))


def _hgrn_conv(proj, ops, gnorm, conv_w, conv_b, layer, batch, t_len):
    assert (t_len // HG_C) % HG_UNROLL == 0 and MIX_W // CV_GROUP == HG_HEADS
    assert HG_UNROLL == SUBLANES
    nc = t_len // HG_C
    qt, kt, kh, rv, be = ops
    spec = lambda c0: pl.BlockSpec((t_len, LANES), lambda b, h: (b, c0 + h))
    both = lambda rows: [pl.BlockSpec((rows, LANES), lambda b, h: (b, h)),
                         pl.BlockSpec((rows, LANES), lambda b, h: (b, HG_HEADS + h))]
    par = lambda rows: pl.BlockSpec((None, rows, LANES), lambda b, h: (layer, 0, h))
    out = jax.ShapeDtypeStruct((batch * t_len, MIX_W), BF16)
    return pl.pallas_call(
        _hgrn_conv_body,
        grid=(batch, HG_HEADS),
        in_specs=[spec(_COL["hi"]), spec(_COL["hg"]), spec(_COL["ca"]), spec(_COL["cb"]),
                  spec(_COL["cc"])]
        + both(t_len) + both(t_len) + both(t_len) + both(t_len // HG_SB) + both(nc)
        + [par(1), par(3), par(1)],
        out_specs=[pl.BlockSpec((t_len, LANES), lambda b, h: (b, h))] * 2,
        out_shape=[out, out],
        scratch_shapes=[pltpu.VMEM((nc, HG_DK, 2 * HG_DK), BF16),
                        pltpu.VMEM((HG_DK, HG_DK), F32), pltpu.VMEM((HG_DK, HG_DK), F32),
                        pltpu.VMEM((2 * HG_UNROLL, HG_C, HG_NSB * HG_C), BF16),
                        pltpu.VMEM((2 * HG_UNROLL, HG_C, HG_DK), F32)],
        compiler_params=_cparams(("parallel", "parallel")),
    )(proj, proj, proj, proj, proj, qt, qt, kt, kt, kh, kh, rv, rv, be, be, gnorm, conv_w, conv_b)


def _na_bias_table(rpb):
    n_layers, n_dr = rpb.shape[0], rpb.shape[2]
    qc = np.arange(GRID_W)[:, None]
    kc = np.arange(GRID_W)[None, :]
    cs = np.clip(qc - NA_KW // 2, 0, GRID_W - NA_KW)
    col_ok = (kc >= cs) & (kc < cs + NA_KW)
    dc = np.clip(kc - qc + NA_KW - 1, 0, 2 * NA_KW - 2)
    pick = (dc[None] == np.arange(2 * NA_KW - 1)[:, None, None]) & col_ok[None]
    t = jnp.einsum("lhrc,cqk->lhrqk", rpb.astype(F32), jnp.asarray(pick, F32),
                   precision=lax.Precision.HIGHEST)
    t = t + jnp.asarray(np.where(col_ok, 0.0, MASK_VALUE), F32)
    t = t.reshape(n_layers, NA_HEADS // 2, 2, n_dr, GRID_W, GRID_W).transpose(0, 1, 3, 2, 4, 5)
    t = t.reshape(n_layers, NA_HEADS // 2, n_dr, 2 * GRID_W, GRID_W)
    return jnp.concatenate([t[:, :, :-1], t[:, :, 1:]], axis=-1)


def _na_body(q_ref, k_ref, v_ref, bias_ref, o_ref):
    n_rows = q_ref.shape[0] // GRID_W
    lane = lax.broadcasted_iota(jnp.int32, (GRID_W, LANES), 1)
    first = lane < NA_DH
    m0 = jnp.where(first, NA_SCALE, 0.0)
    m1 = jnp.where(first, 0.0, NA_SCALE)
    win = NA_KH * GRID_W

    def window(r):
        rs = jnp.clip(r - NA_KH // 2, 0, n_rows - NA_KH)
        return rs, pl.ds(pl.multiple_of(rs * GRID_W, GRID_W), win)

    def logits(r):
        rs, krows = window(r)
        q = q_ref[pl.ds(pl.multiple_of(r * GRID_W, GRID_W), GRID_W), :].astype(F32)
        q2 = jnp.concatenate([q * m0, q * m1], axis=0).astype(BF16)
        first_dr = NA_KH - 1 - (r - rs)
        bias = jnp.concatenate([bias_ref[first_dr + 2 * m] for m in range(NA_KH // 2)], axis=1)
        return lax.dot_general(q2, k_ref[krows, :], _NT, preferred_element_type=F32) + bias

    def attend(s, vw):
        p = jnp.exp(s - jnp.max(s, axis=-1, keepdims=True))
        l = jnp.sum(p, axis=-1, keepdims=True)
        o2 = jnp.dot(p.astype(BF16), vw, preferred_element_type=F32) / l
        return jnp.where(first, o2[:GRID_W, :], o2[GRID_W:, :]).astype(BF16)

    def rows_step(i, carry):
        rows = [i * NA_UNROLL + u for u in range(NA_UNROLL)]
        scores = [logits(r) for r in rows]
        outs = [attend(s, v_ref[window(r)[1], :]) for s, r in zip(scores, rows)]
        for r, o in zip(rows, outs):
            o_ref[pl.ds(pl.multiple_of(r * GRID_W, GRID_W), GRID_W), :] = o
        return carry

    lax.fori_loop(0, n_rows // NA_UNROLL, rows_step, 0)


def _natten(proj, bias, layer, batch, t_len):
    assert t_len % GRID_W == 0 and t_len // GRID_W >= NA_KH
    assert (t_len // GRID_W) % NA_UNROLL == 0
    spec = lambda c0: pl.BlockSpec((t_len, LANES), lambda p, b: (b, c0 + p))
    return pl.pallas_call(
        _na_body,
        grid=(NA_HEADS // 2, batch),
        in_specs=[spec(_COL["nq"]), spec(_COL["nk"]), spec(_COL["nv"]),
                  pl.BlockSpec((None, None, 2 * NA_KH - 2, 2 * GRID_W, 2 * GRID_W),
                               lambda p, b: (layer, p, 0, 0, 0))],
        out_specs=pl.BlockSpec((t_len, LANES), lambda p, b: (b, p)),
        out_shape=jax.ShapeDtypeStruct((batch * t_len, MIX_W), BF16),
        compiler_params=_cparams(("parallel", "parallel")),
    )(proj, proj, proj, bias)


def _mix_ffn_body(x_ref, ghg_ref, gcv_ref, gna_ref, ohg_ref, ocv_ref, ona_ref,
                  whg_ref, wcv_ref, wna_ref, wout_ref, g_ref, wg_ref, wu_ref, wd_ref, *rest, final):
    o_ref = rest[-1]
    ys = [jnp.dot(a_ref[...], w_ref[...], preferred_element_type=F32)
          for a_ref, w_ref in ((ohg_ref, whg_ref), (ocv_ref, wcv_ref), (ona_ref, wna_ref))]
    m = sum(jax.nn.sigmoid(g_ref[...].astype(F32)) * y
            for g_ref, y in zip((ghg_ref, gcv_ref, gna_ref), ys))
    x = x_ref[...] + jnp.dot(m.astype(BF16), wout_ref[...], preferred_element_type=F32)
    y = _half_swiglu(x, g_ref, wg_ref, wu_ref, wd_ref)
    if final:
        y = _rms_scale(y, rest[0][...])
    o_ref[...] = y


def _mix_ffn(x, proj, o_hg, o_cv, o_na, w_hg, w_cv, w_na, w_out, gain, w_gu, w_down, layer,
             final_gain=None):
    n, d = x.shape
    final = final_gain is not None
    gate = lambda c0: pl.BlockSpec((TM, d), lambda i: (i, c0 * LANES // d))
    act = pl.BlockSpec((TM, MIX_W), lambda i: (i, 0))
    wmix = _resident((None, MIX_W, d), lambda i: (layer, 0, 0))
    in_specs = [pl.BlockSpec((TM, d), lambda i: (i, 0)),
                gate(_COL["g_hg"]), gate(_COL["g_cv"]), gate(_COL["g_na"]),
                act, act, act, wmix, wmix, wmix,
                _resident((None, d, d), lambda i: (layer, 0, 0))]
    in_specs += _ffn_weight_specs(d, w_down.shape[1], layer)
    args = [x, proj, proj, proj, o_hg, o_cv, o_na, w_hg, w_cv, w_na, w_out,
            gain, w_gu, w_gu, w_down]
    if final:
        in_specs.append(_resident((1, d), lambda i: (0, 0)))
        args.append(final_gain)
    return pl.pallas_call(
        functools.partial(_mix_ffn_body, final=final),
        grid=(n // TM,),
        in_specs=in_specs,
        out_specs=pl.BlockSpec((TM, d), lambda i: (i, 0)),
        out_shape=jax.ShapeDtypeStruct((n, d), F32),
        compiler_params=_cparams(("parallel",)),
    )(*args)


def _lower_bounds(lb_logits):
    p = jax.nn.softmax(lb_logits.astype(F32), axis=1)
    return jnp.cumsum(p, axis=1) - p[:, :1]


def _trunk(x3, p):
    batch, t_len, d = x3.shape
    assert d % LANES == 0 and t_len % HG_C == 0 and (batch * t_len) % TM == 0
    n_layers = p["n_layers"]
    x = x3.reshape(batch * t_len, d)
    for l in range(n_layers):
        x = _ffn(x, p["ffn1_norm"], p["ffn1_w_gu"], p["ffn1_w_down"], l)
        proj, *hg_ops = _inproj(x, p["mix_norm"], p["w_in"], p["lb"], p["run_sums"], l)
        o_hg, o_cv = _hgrn_conv(proj, hg_ops, p["hg_out_norm"], p["conv_w"], p["conv_b"],
                                l, batch, t_len)
        o_na = _natten(proj, p["na_bias"], l, batch, t_len)
        x = _mix_ffn(x, proj, o_hg, o_cv, o_na, p["w_hg_out"], p["w_cv_out"], p["w_na_out"],
                     p["w_out"], p["ffn2_norm"], p["ffn2_w_gu"], p["ffn2_w_down"], l,
                     final_gain=p["final_norm"] if l == n_layers - 1 else None)
    return x.reshape(batch, t_len, d)


def kernel(x_prompt, x_sample, ffn1_norm, ffn1_w_gu, ffn1_w_down, mix_norm, w_in, hg_lb_logits, hg_out_norm, w_hg_out, conv_w, conv_b, w_cv_out, na_rpb, w_na_out, w_out, ffn2_norm, ffn2_w_gu, ffn2_w_down, final_norm):
    n_layers, d, n_in = w_in.shape
    assert n_in == N_IN
    row = lambda a: a.astype(F32)[:, None, :]
    p = dict(
        n_layers=n_layers,
        ffn1_norm=row(ffn1_norm), ffn1_w_gu=ffn1_w_gu.astype(BF16), ffn1_w_down=ffn1_w_down.astype(BF16),
        ffn2_norm=row(ffn2_norm), ffn2_w_gu=ffn2_w_gu.astype(BF16), ffn2_w_down=ffn2_w_down.astype(BF16),
        mix_norm=row(mix_norm),
        w_in=jnp.concatenate([w_in[..., _SRC[m][0] * LANES:_SRC[m][1] * LANES] for m in _ORDER],
                             axis=-1).astype(BF16),
        lb=_lower_bounds(hg_lb_logits).transpose(1, 0, 2),
        run_sums=_running_sum_matrix(),
        hg_out_norm=row(hg_out_norm),
        w_hg_out=w_hg_out.astype(BF16), w_cv_out=w_cv_out.astype(BF16), w_na_out=w_na_out.astype(BF16),
        conv_w=conv_w.astype(F32), conv_b=row(conv_b),
        na_bias=_na_bias_table(na_rpb),
        w_out=w_out.astype(BF16),
        final_norm=final_norm.astype(F32)[None, :],
    )
    return _trunk(x_prompt, p), _trunk(x_sample, p)
```

```python
import functools

import numpy as np
import jax
import jax.numpy as jnp
from jax import lax
from jax.experimental import pallas as pl
from jax.experimental.pallas import tpu as pltpu

F32 = jnp.float32
BF16 = jnp.bfloat16

GRID_W = 64
RMS_EPS = 1e-6
F_FLOOR = 1e-30
MASK_VALUE = -1e30
HG_HEADS, HG_DK = 4, 128
CV_GROUP = 128
NA_HEADS, NA_DH = 8, 64
NA_KH, NA_KW = 8, 16
NA_SCALE = NA_DH ** -0.5
MIX_W = HG_HEADS * HG_DK

LANES = 128
SUBLANES = 8
MXU_N = 256
VMEM_LIMIT = 56 * 1024 * 1024

TM = 512
TM_FFN = 1024
N_CHUNK = 2 * MXU_N
HG_C = 64
HG_SB = 16
HG_NSB = HG_C // HG_SB
HG_UNROLL = 8
CV_ROWS = 256
NA_UNROLL = 16

_SRC = dict(hq=(0, 4), hi=(4, 8), hzf=(8, 12), hzb=(12, 16), hg=(16, 20), ca=(20, 24), cb=(24, 28),
            cc=(28, 32), nq=(32, 36), nk=(36, 40), nv=(40, 44), g_hg=(44, 52), g_cv=(52, 60),
            g_na=(60, 68))
_ORDER = ("g_hg", "g_cv", "g_na", "hi", "hg", "ca", "cb", "cc", "nq", "nk", "nv", "hq", "hzf", "hzb")
_COL = {}
for _name in _ORDER:
    _COL[_name] = sum(_SRC[m][1] - _SRC[m][0] for m in _ORDER[:_ORDER.index(_name)])
N_IN = 68 * LANES
N_STORE = _COL["hq"] * LANES

_NT = (((1,), (1,)), ((), ()))
_TN = (((0,), (0,)), ((), ()))


def _cparams(sem):
    return pltpu.CompilerParams(dimension_semantics=sem, vmem_limit_bytes=VMEM_LIMIT)


def _resident(block_shape, index_map):
    return pl.BlockSpec(block_shape, index_map, pipeline_mode=pl.Buffered(1))


def _rms_scale(x, gain):
    return x * lax.rsqrt(jnp.mean(x * x, axis=-1, keepdims=True) + RMS_EPS) * gain


def _col_chunks(n):
    return [(c, min(c + N_CHUNK, n)) for c in range(0, n, N_CHUNK)]


def _half_swiglu(x, g_ref, wg_ref, wu_ref, wd_ref):
    xn = _rms_scale(x, g_ref[...]).astype(BF16)

    def hidden(c0, c1):
        a = jnp.dot(xn, wg_ref[:, c0:c1], preferred_element_type=F32)
        b = jnp.dot(xn, wu_ref[:, c0:c1], preferred_element_type=F32)
        return (a * jax.nn.sigmoid(a) * b).astype(BF16)

    chunks = _col_chunks(wd_ref.shape[0])
    acc = None
    h = hidden(*chunks[0])
    for idx, (c0, c1) in enumerate(chunks):
        h_next = hidden(*chunks[idx + 1]) if idx + 1 < len(chunks) else None
        part = jnp.dot(h, wd_ref[c0:c1, :], preferred_element_type=F32)
        acc = part if acc is None else acc + part
        h = h_next
    return x + 0.5 * acc


def _ffn_weight_specs(d, d_ff, layer):
    return [_resident((None, 1, d), lambda i: (layer, 0, 0)),
            _resident((None, d, d_ff), lambda i: (layer, 0, 0)),
            _resident((None, d, d_ff), lambda i: (layer, 0, 1)),
            _resident((None, d_ff, d), lambda i: (layer, 0, 0))]


def _ffn_body(x_ref, g_ref, wg_ref, wu_ref, wd_ref, o_ref):
    o_ref[...] = _half_swiglu(x_ref[...], g_ref, wg_ref, wu_ref, wd_ref)


def _ffn(x, gain, w_gu, w_down, layer):
    n, d = x.shape
    assert n % TM_FFN == 0
    return pl.pallas_call(
        _ffn_body,
        grid=(n // TM_FFN,),
        in_specs=[pl.BlockSpec((TM_FFN, d), lambda i: (i, 0))]
        + _ffn_weight_specs(d, w_down.shape[1], layer),
        out_specs=pl.BlockSpec((TM_FFN, d), lambda i: (i, 0)),
        out_shape=jax.ShapeDtypeStruct((n, d), F32),
        compiler_params=_cparams(("parallel",)),
    )(x, gain, w_gu, w_gu, w_down)


def _running_sum_matrix():
    tok = np.arange(TM)
    chunk, pos = tok // HG_C, tok % HG_C
    ref_tok = np.arange(TM // HG_SB) * HG_SB + HG_SB // 2
    mats = []
    for rev in (False, True):
        end_tok = np.arange(TM // HG_C) * HG_C + (0 if rev else HG_C - 1)
        at = np.concatenate([tok, ref_tok, end_tok])
        same = chunk[None, :] == chunk[at][:, None]
        upto = pos[None, :] >= pos[at][:, None] if rev else pos[None, :] <= pos[at][:, None]
        pad = np.zeros((-len(at) % (2 * SUBLANES), TM), bool)
        mats.append(np.concatenate([same & upto, pad], axis=0))
    return jnp.asarray(np.stack(mats).astype(np.float32), BF16)


def _inproj_body(x_ref, g_ref, w_ref, lb_ref, sum_ref,
                 o_ref, qt_ref, kt_ref, kh_ref, rv_ref, be_ref):
    xn = _rms_scale(x_ref[...], g_ref[...]).astype(BF16)
    n_store = o_ref.shape[1]
    w = kt_ref.shape[1] // 2

    def proj(c0, c1):
        return jnp.dot(xn, w_ref[:, c0:c1], preferred_element_type=F32)

    q = proj(n_store, n_store + w)
    gated = []
    for side in range(2):
        z = proj(n_store + (1 + side) * w, n_store + (2 + side) * w)
        lb = lb_ref[side:side + 1, :]
        f = lb + (1.0 - lb) * jax.nn.sigmoid(z)
        gated.append((1.0 - f, jnp.log(jnp.maximum(f, F_FLOOR)).astype(BF16)))
    pieces = [(side, c0) for side in range(2) for c0 in range(0, w, MXU_N)]
    slab = _col_chunks(n_store)
    every = len(slab) // (len(pieces) + 1)
    sums = {}
    for idx, (c0, c1) in enumerate(slab):
        o_ref[:, c0:c1] = proj(c0, c1).astype(BF16)
        if (idx + 1) % every == 0 and len(sums) < len(pieces):
            side, p0 = pieces[len(sums)]
            logf = gated[side][1][:, p0:p0 + MXU_N]
            parts = [jnp.dot(sum_ref[side, r0:r0 + LANES, r0:r0 + LANES], logf[r0:r0 + LANES],
                             preferred_element_type=F32) for r0 in range(0, TM, LANES)]
            parts.append(jnp.dot(sum_ref[side, TM:, :], logf, preferred_element_type=F32))
            sums[side, p0] = jnp.concatenate(parts, axis=0)
    n_r, n_e = TM // HG_SB, TM // HG_C
    for (side, p0), full in sums.items():
        k = gated[side][0][:, p0:p0 + MXU_N]
        b = full[:TM]
        r_rows, e_rows = [], []
        for r0 in range(0, TM, HG_C):
            e = b[r0:r0 + 1, :] if side else b[r0 + HG_C - 1:r0 + HG_C, :]
            e_rows.append(jnp.broadcast_to(e, (HG_C, MXU_N)))
            for s0 in range(r0, r0 + HG_C, HG_SB):
                r = b[s0 + HG_SB // 2:s0 + HG_SB // 2 + 1, :]
                r_rows.append(jnp.broadcast_to(r, (HG_SB, MXU_N)))
        r_rows = jnp.concatenate(r_rows, axis=0)
        e_rows = jnp.concatenate(e_rows, axis=0)
        d = r_rows - b
        cols = slice(side * w + p0, side * w + p0 + MXU_N)
        kt_ref[:, cols] = (k * jnp.exp(d)).astype(BF16)
        qt_ref[:, cols] = (q[:, p0:p0 + MXU_N] * jnp.exp(-d)).astype(BF16)
        kh_ref[:, cols] = (k * jnp.exp(e_rows - b)).astype(BF16)
        rv_ref[:, cols] = full[TM:TM + n_r]
        be_ref[:, cols] = full[TM + n_r:TM + n_r + n_e]


def _inproj(x, gain, w_in, lb, sums, layer):
    n, d = x.shape
    n_in = w_in.shape[-1]
    w2 = 2 * MIX_W
    rows = lambda r: pl.BlockSpec((r, w2), lambda i: (i, 0))
    return pl.pallas_call(
        _inproj_body,
        grid=(n // TM,),
        in_specs=[
            pl.BlockSpec((TM, d), lambda i: (i, 0)),
            _resident((None, 1, d), lambda i: (layer, 0, 0)),
            _resident((None, d, n_in), lambda i: (layer, 0, 0)),
            _resident((None, 2, MIX_W), lambda i: (layer, 0, 0)),
            _resident(sums.shape, lambda i: (0, 0, 0)),
        ],
        out_specs=[pl.BlockSpec((TM, N_STORE), lambda i: (i, 0)), rows(TM), rows(TM), rows(TM),
                   rows(TM // HG_SB), rows(TM // HG_C)],
        out_shape=[jax.ShapeDtypeStruct((n, N_STORE), BF16)]
        + [jax.ShapeDtypeStruct((n, w2), BF16)] * 3
        + [jax.ShapeDtypeStruct((n // HG_SB, w2), F32),
           jax.ShapeDtypeStruct((n // HG_C, w2), F32)],
        compiler_params=_cparams(("parallel",)),
    )(x, gain, w_in, lb, sums)


def _per_block(vecs):
    return jnp.concatenate([jnp.broadcast_to(v, (HG_SB, v.shape[1])) for v in vecs], axis=0)


def _key_variants(kt, refs, rev):
    blocks = [kt[HG_SB * j:HG_SB * (j + 1), :] for j in range(HG_NSB)]
    wide = [blk.astype(F32) for blk in blocks]
    zero = jnp.zeros(blocks[0].shape, BF16)
    parts = []
    for i in range(HG_NSB):
        for j in range(HG_NSB):
            if j == i:
                parts.append(blocks[j])
            elif (j > i) if rev else (j < i):
                parts.append((wide[j] * jnp.exp(refs[i] - refs[j])).astype(BF16))
            else:
                parts.append(zero)
    return jnp.concatenate(parts, axis=0)


def _conv_gate(ca_ref, cb_ref, cc_ref, w_ref, bias_ref, o_ref, win_ref):
    t_len = ca_ref.shape[0]
    halo = 2 * SUBLANES
    n = CV_ROWS + 2 * halo
    assert t_len % CV_ROWS == 0 and t_len >= n
    edge = lax.broadcasted_iota(jnp.int32, (SUBLANES, ca_ref.shape[1]), 0)

    def window(i, carry):
        r0 = i * CV_ROWS
        start = pl.multiple_of(jnp.clip(r0 - halo, 0, t_len - n), halo)
        rows = pl.ds(start, n)
        z = cc_ref[rows, :].astype(F32) * ca_ref[rows, :].astype(F32)
        prev = pltpu.roll(z, 1, axis=0)
        prev = jnp.concatenate([jnp.where(edge == 0, 0.0, prev[:SUBLANES]), prev[SUBLANES:]], axis=0)
        nxt = pltpu.roll(z, n - 1, axis=0)
        nxt = jnp.concatenate([nxt[:-SUBLANES],
                               jnp.where(edge == SUBLANES - 1, 0.0, nxt[-SUBLANES:])], axis=0)
        zc = prev * w_ref[0:1, :] + z * w_ref[1:2, :] + nxt * w_ref[2:3, :]
        win_ref[...] = cb_ref[rows, :].astype(F32) * (zc + bias_ref[...])
        keep = pl.ds(pl.multiple_of(r0 - start, halo), CV_ROWS)
        o_ref[pl.ds(pl.multiple_of(r0, CV_ROWS), CV_ROWS), :] = win_ref[keep, :].astype(BF16)
        return carry

    lax.fori_loop(0, t_len // CV_ROWS, window, 0)


def _hgrn_conv_body(v_ref, g_ref, ca_ref, cb_ref, cc_ref,
                    qtf_ref, qtb_ref, ktf_ref, ktb_ref, khf_ref, khb_ref,
                    rvf_ref, rvb_ref, bef_ref, beb_ref, gn_ref, cw_ref, cbias_ref,
                    o_ref, ocv_ref, s_ref, stf_ref, stb_ref, rm_ref, oi_ref, win_ref):
    t_len = v_ref.shape[0]
    nc = t_len // HG_C
    w = HG_DK
    tr = lax.broadcasted_iota(jnp.int32, (HG_C, HG_NSB * HG_C), 0)
    cr = lax.broadcasted_iota(jnp.int32, (HG_C, HG_NSB * HG_C), 1)
    own = (cr // HG_C) == (tr // HG_SB)
    src = cr % HG_C
    rmask_f = own & (src <= tr)
    rmask_b = own & (src >= tr)
    side = {False: (qtf_ref, ktf_ref, khf_ref, rvf_ref, bef_ref),
            True: (qtb_ref, ktb_ref, khb_ref, rvb_ref, beb_ref)}

    _conv_gate(ca_ref, cb_ref, cc_ref, cw_ref, cbias_ref, ocv_ref, win_ref)

    def chunk_rows(c):
        return pl.ds(pl.multiple_of(c * HG_C, HG_C), HG_C)

    def park(refs, slot, items):
        for idx, vals in enumerate(items):
            for ref, val in zip(refs, vals):
                ref[slot + idx] = val

    n_blk = nc // HG_UNROLL

    def pass1(i, carry):
        first = i * HG_UNROLL
        last = nc - HG_UNROLL - first
        decs = {False: jnp.exp(bef_ref[pl.ds(pl.multiple_of(first, HG_UNROLL), HG_UNROLL), :]),
                True: jnp.exp(beb_ref[pl.ds(pl.multiple_of(last, HG_UNROLL), HG_UNROLL), :])}
        steps = []
        for u in range(HG_UNROLL):
            for rev, c, row in ((False, first + u, u), (True, nc - 1 - first - u, HG_UNROLL - 1 - u)):
                upd = lax.dot_general(v_ref[chunk_rows(c), :], side[rev][2][chunk_rows(c), :], _TN,
                                      preferred_element_type=F32)
                steps.append((rev, c, decs[rev][row:row + 1, :], upd))
        st = {False: stf_ref[...], True: stb_ref[...]}
        for rev, c, dec, upd in steps:
            s_ref[c, :, (w if rev else 0):(2 * w if rev else w)] = st[rev].astype(BF16)
            st[rev] = st[rev] * dec + upd
        stf_ref[...] = st[False]
        stb_ref[...] = st[True]
        return carry

    stf_ref[...] = jnp.zeros_like(stf_ref)
    stb_ref[...] = jnp.zeros_like(stb_ref)
    lax.fori_loop(0, n_blk, pass1, 0)

    def scores(blk):
        pre = []
        blk_refs = pl.ds(pl.multiple_of(blk * (HG_UNROLL * HG_NSB), HG_UNROLL * HG_NSB),
                         HG_UNROLL * HG_NSB)
        rvs = {rev: side[rev][3][blk_refs, :] for rev in (False, True)}
        for u in range(HG_UNROLL):
            c = blk * HG_UNROLL + u
            rows = chunk_rows(c)
            rs, qhs = [], []
            for rev in (False, True):
                qt_ref, kt_ref = side[rev][:2]
                refs = [rvs[rev][HG_NSB * u + j:HG_NSB * u + j + 1, :] for j in range(HG_NSB)]
                qt = qt_ref[rows, :]
                kall = _key_variants(kt_ref[rows, :], refs, rev)
                rs.append(lax.dot_general(qt, kall, _NT, preferred_element_type=F32))
                qhs.append(qt.astype(F32) * _per_block([jnp.exp(r) for r in refs]))
            pre.append((c, rs, jnp.concatenate(qhs, axis=1).astype(BF16)))
        out = []
        for c, (r_f, r_b), qh in pre:
            oi = lax.dot_general(qh, s_ref[c], _NT, preferred_element_type=F32)
            rm = (jnp.where(rmask_f, r_f, 0.0) + jnp.where(rmask_b, r_b, 0.0)).astype(BF16)
            out.append((rm, oi))
        return out

    park((rm_ref, oi_ref), 0, scores(0))

    def outputs(i, slot):
        outs = []
        for u in range(HG_UNROLL):
            rows = chunk_rows(i * HG_UNROLL + u)
            v4 = jnp.concatenate([v_ref[rows, :]] * HG_NSB, axis=0)
            o = oi_ref[slot + u] + jnp.dot(rm_ref[slot + u], v4, preferred_element_type=F32)
            g = g_ref[rows, :].astype(F32)
            o = _rms_scale(o, gn_ref[...]) * (g * jax.nn.sigmoid(g))
            outs.append((rows, o.astype(BF16)))
        return outs

    def commit_outputs(outs):
        for rows, o in outs:
            o_ref[rows, :] = o

    def pass2(i, carry):
        cur = (i & 1) * HG_UNROLL
        outs = outputs(i, cur)
        fresh = scores(i + 1)
        commit_outputs(outs)
        park((rm_ref, oi_ref), HG_UNROLL - cur, fresh)
        return carry

    lax.fori_loop(0, n_blk - 1, pass2, 0)
    commit_outputs(outputs(n_blk - 1, ((n_blk - 1) & 1) * HG_UNROLL))


def _hgrn_conv(proj, ops, gnorm, conv_w, conv_b, layer, batch, t_len):
    assert (t_len // HG_C) % HG_UNROLL == 0 and MIX_W // CV_GROUP == HG_HEADS
    assert HG_UNROLL == SUBLANES
    nc = t_len // HG_C
    qt, kt, kh, rv, be = ops
    spec = lambda c0: pl.BlockSpec((t_len, LANES), lambda b, h: (b, c0 + h))
    both = lambda rows: [pl.BlockSpec((rows, LANES), lambda b, h: (b, h)),
                         pl.BlockSpec((rows, LANES), lambda b, h: (b, HG_HEADS + h))]
    par = lambda rows: pl.BlockSpec((None, rows, LANES), lambda b, h: (layer, 0, h))
    out = jax.ShapeDtypeStruct((batch * t_len, MIX_W), BF16)
    return pl.pallas_call(
        _hgrn_conv_body,
        grid=(batch, HG_HEADS),
        in_specs=[spec(_COL["hi"]), spec(_COL["hg"]), spec(_COL["ca"]), spec(_COL["cb"]),
                  spec(_COL["cc"])]
        + both(t_len) + both(t_len) + both(t_len) + both(t_len // HG_SB) + both(nc)
        + [par(1), par(3), par(1)],
        out_specs=[pl.BlockSpec((t_len, LANES), lambda b, h: (b, h))] * 2,
        out_shape=[out, out],
        scratch_shapes=[pltpu.VMEM((nc, HG_DK, 2 * HG_DK), BF16),
                        pltpu.VMEM((HG_DK, HG_DK), F32), pltpu.VMEM((HG_DK, HG_DK), F32),
                        pltpu.VMEM((2 * HG_UNROLL, HG_C, HG_NSB * HG_C), BF16),
                        pltpu.VMEM((2 * HG_UNROLL, HG_C, HG_DK), F32),
                        pltpu.VMEM((CV_ROWS + 4 * SUBLANES, CV_GROUP), F32)],
        compiler_params=_cparams(("parallel", "parallel")),
    )(proj, proj, proj, proj, proj, qt, qt, kt, kt, kh, kh, rv, rv, be, be, gnorm, conv_w, conv_b)


def _na_bias_table(rpb):
    n_layers, n_dr = rpb.shape[0], rpb.shape[2]
    qc = np.arange(GRID_W)[:, None]
    kc = np.arange(GRID_W)[None, :]
    cs = np.clip(qc - NA_KW // 2, 0, GRID_W - NA_KW)
    col_ok = (kc >= cs) & (kc < cs + NA_KW)
    dc = np.clip(kc - qc + NA_KW - 1, 0, 2 * NA_KW - 2)
    pick = (dc[None] == np.arange(2 * NA_KW - 1)[:, None, None]) & col_ok[None]
    t = jnp.einsum("lhrc,cqk->lhrqk", rpb.astype(F32), jnp.asarray(pick, F32),
                   precision=lax.Precision.HIGHEST)
    t = t + jnp.asarray(np.where(col_ok, 0.0, MASK_VALUE), F32)
    t = t.reshape(n_layers, NA_HEADS // 2, 2, n_dr, GRID_W, GRID_W).transpose(0, 1, 3, 2, 4, 5)
    t = t.reshape(n_layers, NA_HEADS // 2, n_dr, 2 * GRID_W, GRID_W)
    return jnp.concatenate([t[:, :, :-1], t[:, :, 1:]], axis=-1)


def _na_body(q_ref, k_ref, v_ref, bias_ref, o_ref):
    n_rows = q_ref.shape[0] // GRID_W
    lane = lax.broadcasted_iota(jnp.int32, (GRID_W, LANES), 1)
    first = lane < NA_DH
    m0 = jnp.where(first, NA_SCALE, 0.0)
    m1 = jnp.where(first, 0.0, NA_SCALE)
    win = NA_KH * GRID_W

    def window(r):
        rs = jnp.clip(r - NA_KH // 2, 0, n_rows - NA_KH)
        return rs, pl.ds(pl.multiple_of(rs * GRID_W, GRID_W), win)

    def logits(r):
        rs, krows = window(r)
        q = q_ref[pl.ds(pl.multiple_of(r * GRID_W, GRID_W), GRID_W), :].astype(F32)
        q2 = jnp.concatenate([q * m0, q * m1], axis=0).astype(BF16)
        first_dr = NA_KH - 1 - (r - rs)
        bias = jnp.concatenate([bias_ref[first_dr + 2 * m] for m in range(NA_KH // 2)], axis=1)
        return lax.dot_general(q2, k_ref[krows, :], _NT, preferred_element_type=F32) + bias

    def attend(s, vw):
        p = jnp.exp(s - jnp.max(s, axis=-1, keepdims=True))
        l = jnp.sum(p, axis=-1, keepdims=True)
        o2 = jnp.dot(p.astype(BF16), vw, preferred_element_type=F32) / l
        return jnp.where(first, o2[:GRID_W, :], o2[GRID_W:, :]).astype(BF16)

    def rows_step(i, carry):
        rows = [i * NA_UNROLL + u for u in range(NA_UNROLL)]
        scores = [logits(r) for r in rows]
        outs = [attend(s, v_ref[window(r)[1], :]) for s, r in zip(scores, rows)]
        for r, o in zip(rows, outs):
            o_ref[pl.ds(pl.multiple_of(r * GRID_W, GRID_W), GRID_W), :] = o
        return carry

    lax.fori_loop(0, n_rows // NA_UNROLL, rows_step, 0)


def _natten(proj, bias, layer, batch, t_len):
    assert t_len % GRID_W == 0 and t_len // GRID_W >= NA_KH
    assert (t_len // GRID_W) % NA_UNROLL == 0
    spec = lambda c0: pl.BlockSpec((t_len, LANES), lambda p, b: (b, c0 + p))
    return pl.pallas_call(
        _na_body,
        grid=(NA_HEADS // 2, batch),
        in_specs=[spec(_COL["nq"]), spec(_COL["nk"]), spec(_COL["nv"]),
                  pl.BlockSpec((None, None, 2 * NA_KH - 2, 2 * GRID_W, 2 * GRID_W),
                               lambda p, b: (layer, p, 0, 0, 0))],
        out_specs=pl.BlockSpec((t_len, LANES), lambda p, b: (b, p)),
        out_shape=jax.ShapeDtypeStruct((batch * t_len, MIX_W), BF16),
        compiler_params=_cparams(("parallel", "parallel")),
    )(proj, proj, proj, bias)


def _mix_ffn_body(x_ref, ghg_ref, gcv_ref, gna_ref, ohg_ref, ocv_ref, ona_ref,
                  whg_ref, wcv_ref, wna_ref, wout_ref, g_ref, wg_ref, wu_ref, wd_ref, *rest, final):
    o_ref = rest[-1]
    ys = [jnp.dot(a_ref[...], w_ref[...], preferred_element_type=F32)
          for a_ref, w_ref in ((ohg_ref, whg_ref), (ocv_ref, wcv_ref), (ona_ref, wna_ref))]
    m = sum(jax.nn.sigmoid(g_ref[...].astype(F32)) * y
            for g_ref, y in zip((ghg_ref, gcv_ref, gna_ref), ys))
    x = x_ref[...] + jnp.dot(m.astype(BF16), wout_ref[...], preferred_element_type=F32)
    y = _half_swiglu(x, g_ref, wg_ref, wu_ref, wd_ref)
    if final:
        y = _rms_scale(y, rest[0][...])
    o_ref[...] = y


def _mix_ffn(x, proj, o_hg, o_cv, o_na, w_hg, w_cv, w_na, w_out, gain, w_gu, w_down, layer,
             final_gain=None):
    n, d = x.shape
    final = final_gain is not None
    gate = lambda c0: pl.BlockSpec((TM, d), lambda i: (i, c0 * LANES // d))
    act = pl.BlockSpec((TM, MIX_W), lambda i: (i, 0))
    wmix = _resident((None, MIX_W, d), lambda i: (layer, 0, 0))
    in_specs = [pl.BlockSpec((TM, d), lambda i: (i, 0)),
                gate(_COL["g_hg"]), gate(_COL["g_cv"]), gate(_COL["g_na"]),
                act, act, act, wmix, wmix, wmix,
                _resident((None, d, d), lambda i: (layer, 0, 0))]
    in_specs += _ffn_weight_specs(d, w_down.shape[1], layer)
    args = [x, proj, proj, proj, o_hg, o_cv, o_na, w_hg, w_cv, w_na, w_out,
            gain, w_gu, w_gu, w_down]
    if final:
        in_specs.append(_resident((1, d), lambda i: (0, 0)))
        args.append(final_gain)
    return pl.pallas_call(
        functools.partial(_mix_ffn_body, final=final),
        grid=(n // TM,),
        in_specs=in_specs,
        out_specs=pl.BlockSpec((TM, d), lambda i: (i, 0)),
        out_shape=jax.ShapeDtypeStruct((n, d), F32),
        compiler_params=_cparams(("parallel",)),
    )(*args)


def _lower_bounds(lb_logits):
    p = jax.nn.softmax(lb_logits.astype(F32), axis=1)
    return jnp.cumsum(p, axis=1) - p[:, :1]


def _trunk(x3, p):
    batch, t_len, d = x3.shape
    assert d % LANES == 0 and t_len % HG_C == 0 and (batch * t_len) % TM == 0
    n_layers = p["n_layers"]
    x = x3.reshape(batch * t_len, d)
    for l in range(n_layers):
        x = _ffn(x, p["ffn1_norm"], p["ffn1_w_gu"], p["ffn1_w_down"], l)
        proj, *hg_ops = _inproj(x, p["mix_norm"], p["w_in"], p["lb"], p["run_sums"], l)
        o_hg, o_cv = _hgrn_conv(proj, hg_ops, p["hg_out_norm"], p["conv_w"], p["conv_b"],
                                l, batch, t_len)
        o_na = _natten(proj, p["na_bias"], l, batch, t_len)
        x = _mix_ffn(x, proj, o_hg, o_cv, o_na, p["w_hg_out"], p["w_cv_out"], p["w_na_out"],
                     p["w_out"], p["ffn2_norm"], p["ffn2_w_gu"], p["ffn2_w_down"], l,
                     final_gain=p["final_norm"] if l == n_layers - 1 else None)
    return x.reshape(batch, t_len, d)


def kernel(x_prompt, x_sample, ffn1_norm, ffn1_w_gu, ffn1_w_down, mix_norm, w_in, hg_lb_logits, hg_out_norm, w_hg_out, conv_w, conv_b, w_cv_out, na_rpb, w_na_out, w_out, ffn2_norm, ffn2_w_gu, ffn2_w_down, final_norm):
    n_layers, d, n_in = w_in.shape
    assert n_in == N_IN
    row = lambda a: a.astype(F32)[:, None, :]
    p = dict(
        n_layers=n_layers,
        ffn1_norm=row(ffn1_norm), ffn1_w_gu=ffn1_w_gu.astype(BF16), ffn1_w_down=ffn1_w_down.astype(BF16),
        ffn2_norm=row(ffn2_norm), ffn2_w_gu=ffn2_w_gu.astype(BF16), ffn2_w_down=ffn2_w_down.astype(BF16),
        mix_norm=row(mix_norm),
        w_in=jnp.concatenate([w_in[..., _SRC[m][0] * LANES:_SRC[m][1] * LANES] for m in _ORDER],
                             axis=-1).astype(BF16),
        lb=_lower_bounds(hg_lb_logits).transpose(1, 0, 2),
        run_sums=_running_sum_matrix(),
        hg_out_norm=row(hg_out_norm),
        w_hg_out=w_hg_out.astype(BF16), w_cv_out=w_cv_out.astype(BF16), w_na_out=w_na_out.astype(BF16),
        conv_w=conv_w.astype(F32), conv_b=row(conv_b),
        na_bias=_na_bias_table(na_rpb),
        w_out=w_out.astype(BF16),
        final_norm=final_norm.astype(F32)[None, :],
    )
    return _trunk(x_prompt, p), _trunk(x_sample, p)
```

```python
import functools

import numpy as np
import jax
import jax.numpy as jnp
from jax import lax
from jax.experimental import pallas as pl
from jax.experimental.pallas import tpu as pltpu

F32 = jnp.float32
BF16 = jnp.bfloat16

GRID_W = 64
RMS_EPS = 1e-6
F_FLOOR = 1e-30
MASK_VALUE = -1e30
HG_HEADS, HG_DK = 4, 128
CV_GROUP = 128
NA_HEADS, NA_DH = 8, 64
NA_KH, NA_KW = 8, 16
NA_SCALE = NA_DH ** -0.5
MIX_W = HG_HEADS * HG_DK

LANES = 128
SUBLANES = 8
MXU_N = 256
VMEM_LIMIT = 56 * 1024 * 1024

TM = 512
TM_FFN = 1024
N_CHUNK = 2 * MXU_N
HG_C = 64
HG_SB = 16
HG_NSB = HG_C // HG_SB
HG_UNROLL = 8
NA_UNROLL = 32

_SRC = dict(hq=(0, 4), hi=(4, 8), hzf=(8, 12), hzb=(12, 16), hg=(16, 20), ca=(20, 24), cb=(24, 28),
            cc=(28, 32), nq=(32, 36), nk=(36, 40), nv=(40, 44), g_hg=(44, 52), g_cv=(52, 60),
            g_na=(60, 68))
_ORDER = ("g_hg", "g_cv", "g_na", "hi", "hg", "ca", "cb", "cc", "nq", "nk", "nv", "hq", "hzf", "hzb")
_COL = {}
for _name in _ORDER:
    _COL[_name] = sum(_SRC[m][1] - _SRC[m][0] for m in _ORDER[:_ORDER.index(_name)])
N_IN = 68 * LANES
N_STORE = _COL["hq"] * LANES

_NT = (((1,), (1,)), ((), ()))
_TN = (((0,), (0,)), ((), ()))


def _cparams(sem):
    return pltpu.CompilerParams(dimension_semantics=sem, vmem_limit_bytes=VMEM_LIMIT)


def _resident(block_shape, index_map):
    return pl.BlockSpec(block_shape, index_map, pipeline_mode=pl.Buffered(1))


def _rms_scale(x, gain):
    return x * lax.rsqrt(jnp.mean(x * x, axis=-1, keepdims=True) + RMS_EPS) * gain


def _col_chunks(n):
    return [(c, min(c + N_CHUNK, n)) for c in range(0, n, N_CHUNK)]


def _half_swiglu(x, g_ref, wg_ref, wu_ref, wd_ref):
    xn = _rms_scale(x, g_ref[...]).astype(BF16)

    def hidden(c0, c1):
        a = jnp.dot(xn, wg_ref[:, c0:c1], preferred_element_type=F32)
        b = jnp.dot(xn, wu_ref[:, c0:c1], preferred_element_type=F32)
        return (a * jax.nn.sigmoid(a) * b).astype(BF16)

    chunks = _col_chunks(wd_ref.shape[0])
    acc = None
    h = hidden(*chunks[0])
    for idx, (c0, c1) in enumerate(chunks):
        h_next = hidden(*chunks[idx + 1]) if idx + 1 < len(chunks) else None
        part = jnp.dot(h, wd_ref[c0:c1, :], preferred_element_type=F32)
        acc = part if acc is None else acc + part
        h = h_next
    return x + 0.5 * acc


def _ffn_weight_specs(d, d_ff, layer):
    return [_resident((None, 1, d), lambda i: (layer, 0, 0)),
            _resident((None, d, d_ff), lambda i: (layer, 0, 0)),
            _resident((None, d, d_ff), lambda i: (layer, 0, 1)),
            _resident((None, d_ff, d), lambda i: (layer, 0, 0))]


def _ffn_body(x_ref, g_ref, wg_ref, wu_ref, wd_ref, o_ref):
    o_ref[...] = _half_swiglu(x_ref[...], g_ref, wg_ref, wu_ref, wd_ref)


def _ffn(x, gain, w_gu, w_down, layer):
    n, d = x.shape
    assert n % TM_FFN == 0
    return pl.pallas_call(
        _ffn_body,
        grid=(n // TM_FFN,),
        in_specs=[pl.BlockSpec((TM_FFN, d), lambda i: (i, 0))]
        + _ffn_weight_specs(d, w_down.shape[1], layer),
        out_specs=pl.BlockSpec((TM_FFN, d), lambda i: (i, 0)),
        out_shape=jax.ShapeDtypeStruct((n, d), F32),
        compiler_params=_cparams(("parallel",)),
    )(x, gain, w_gu, w_gu, w_down)


def _running_sum_matrix():
    tok = np.arange(TM)
    chunk, pos = tok // HG_C, tok % HG_C
    ref_tok = np.arange(TM // HG_SB) * HG_SB + HG_SB // 2
    mats = []
    for rev in (False, True):
        end_tok = np.arange(TM // HG_C) * HG_C + (0 if rev else HG_C - 1)
        at = np.concatenate([tok, ref_tok, end_tok])
        same = chunk[None, :] == chunk[at][:, None]
        upto = pos[None, :] >= pos[at][:, None] if rev else pos[None, :] <= pos[at][:, None]
        pad = np.zeros((-len(at) % (2 * SUBLANES), TM), bool)
        mats.append(np.concatenate([same & upto, pad], axis=0))
    return jnp.asarray(np.stack(mats).astype(np.float32), BF16)


def _inproj_body(x_ref, g_ref, w_ref, lb_ref, sum_ref,
                 o_ref, qt_ref, kt_ref, kh_ref, rv_ref, be_ref):
    xn = _rms_scale(x_ref[...], g_ref[...]).astype(BF16)
    n_store = o_ref.shape[1]
    w = kt_ref.shape[1] // 2

    def proj(c0, c1):
        return jnp.dot(xn, w_ref[:, c0:c1], preferred_element_type=F32)

    q = proj(n_store, n_store + w)
    gated = []
    for side in range(2):
        z = proj(n_store + (1 + side) * w, n_store + (2 + side) * w)
        lb = lb_ref[side:side + 1, :]
        f = lb + (1.0 - lb) * jax.nn.sigmoid(z)
        gated.append((1.0 - f, jnp.log(jnp.maximum(f, F_FLOOR)).astype(BF16)))
    pieces = [(side, c0) for side in range(2) for c0 in range(0, w, MXU_N)]
    slab = _col_chunks(n_store)
    every = len(slab) // (len(pieces) + 1)
    sums = {}
    for idx, (c0, c1) in enumerate(slab):
        o_ref[:, c0:c1] = proj(c0, c1).astype(BF16)
        if (idx + 1) % every == 0 and len(sums) < len(pieces):
            side, p0 = pieces[len(sums)]
            logf = gated[side][1][:, p0:p0 + MXU_N]
            parts = [jnp.dot(sum_ref[side, r0:r0 + LANES, r0:r0 + LANES], logf[r0:r0 + LANES],
                             preferred_element_type=F32) for r0 in range(0, TM, LANES)]
            parts.append(jnp.dot(sum_ref[side, TM:, :], logf, preferred_element_type=F32))
            sums[side, p0] = jnp.concatenate(parts, axis=0)
    n_r, n_e = TM // HG_SB, TM // HG_C
    for (side, p0), full in sums.items():
        k = gated[side][0][:, p0:p0 + MXU_N]
        b = full[:TM]
        r_rows, e_rows = [], []
        for r0 in range(0, TM, HG_C):
            e = b[r0:r0 + 1, :] if side else b[r0 + HG_C - 1:r0 + HG_C, :]
            e_rows.append(jnp.broadcast_to(e, (HG_C, MXU_N)))
            for s0 in range(r0, r0 + HG_C, HG_SB):
                r = b[s0 + HG_SB // 2:s0 + HG_SB // 2 + 1, :]
                r_rows.append(jnp.broadcast_to(r, (HG_SB, MXU_N)))
        r_rows = jnp.concatenate(r_rows, axis=0)
        e_rows = jnp.concatenate(e_rows, axis=0)
        d = r_rows - b
        cols = slice(side * w + p0, side * w + p0 + MXU_N)
        kt_ref[:, cols] = (k * jnp.exp(d)).astype(BF16)
        qt_ref[:, cols] = (q[:, p0:p0 + MXU_N] * jnp.exp(-d)).astype(BF16)
        kh_ref[:, cols] = (k * jnp.exp(e_rows - b)).astype(BF16)
        rv_ref[:, cols] = full[TM:TM + n_r]
        be_ref[:, cols] = full[TM + n_r:TM + n_r + n_e]


def _inproj(x, gain, w_in, lb, sums, layer):
    n, d = x.shape
    n_in = w_in.shape[-1]
    w2 = 2 * MIX_W
    rows = lambda r: pl.BlockSpec((r, w2), lambda i: (i, 0))
    return pl.pallas_call(
        _inproj_body,
        grid=(n // TM,),
        in_specs=[
            pl.BlockSpec((TM, d), lambda i: (i, 0)),
            _resident((None, 1, d), lambda i: (layer, 0, 0)),
            _resident((None, d, n_in), lambda i: (layer, 0, 0)),
            _resident((None, 2, MIX_W), lambda i: (layer, 0, 0)),
            _resident(sums.shape, lambda i: (0, 0, 0)),
        ],
        out_specs=[pl.BlockSpec((TM, N_STORE), lambda i: (i, 0)), rows(TM), rows(TM), rows(TM),
                   rows(TM // HG_SB), rows(TM // HG_C)],
        out_shape=[jax.ShapeDtypeStruct((n, N_STORE), BF16)]
        + [jax.ShapeDtypeStruct((n, w2), BF16)] * 3
        + [jax.ShapeDtypeStruct((n // HG_SB, w2), F32),
           jax.ShapeDtypeStruct((n // HG_C, w2), F32)],
        compiler_params=_cparams(("parallel",)),
    )(x, gain, w_in, lb, sums)


def _per_block(vecs):
    return jnp.concatenate([jnp.broadcast_to(v, (HG_SB, v.shape[1])) for v in vecs], axis=0)


def _key_variants(kt, refs, rev):
    blocks = [kt[HG_SB * j:HG_SB * (j + 1), :] for j in range(HG_NSB)]
    wide = [blk.astype(F32) for blk in blocks]
    zero = jnp.zeros(blocks[0].shape, BF16)
    parts = []
    for i in range(HG_NSB):
        for j in range(HG_NSB):
            if j == i:
                parts.append(blocks[j])
            elif (j > i) if rev else (j < i):
                parts.append((wide[j] * jnp.exp(refs[i] - refs[j])).astype(BF16))
            else:
                parts.append(zero)
    return jnp.concatenate(parts, axis=0)


def _conv_gate(ca_ref, cb_ref, cc_ref, w_ref, bias_ref, o_ref):
    t_len = ca_ref.shape[0]
    z = cc_ref[...].astype(F32) * ca_ref[...].astype(F32)
    edge = lax.broadcasted_iota(jnp.int32, (SUBLANES, z.shape[1]), 0)
    prev = pltpu.roll(z, 1, axis=0)
    prev = jnp.concatenate([jnp.where(edge == 0, 0.0, prev[:SUBLANES]), prev[SUBLANES:]], axis=0)
    nxt = pltpu.roll(z, t_len - 1, axis=0)
    nxt = jnp.concatenate([nxt[:-SUBLANES],
                           jnp.where(edge == SUBLANES - 1, 0.0, nxt[-SUBLANES:])], axis=0)
    zc = prev * w_ref[0:1, :] + z * w_ref[1:2, :] + nxt * w_ref[2:3, :]
    o_ref[...] = (cb_ref[...].astype(F32) * (zc + bias_ref[...])).astype(BF16)


def _hgrn_conv_body(v_ref, g_ref, ca_ref, cb_ref, cc_ref,
                    qtf_ref, qtb_ref, ktf_ref, ktb_ref, khf_ref, khb_ref,
                    rvf_ref, rvb_ref, bef_ref, beb_ref, gn_ref, cw_ref, cbias_ref,
                    o_ref, ocv_ref, s_ref, stf_ref, stb_ref, rm_ref, oi_ref):
    t_len = v_ref.shape[0]
    nc = t_len // HG_C
    w = HG_DK
    tr = lax.broadcasted_iota(jnp.int32, (HG_C, HG_NSB * HG_C), 0)
    cr = lax.broadcasted_iota(jnp.int32, (HG_C, HG_NSB * HG_C), 1)
    own = (cr // HG_C) == (tr // HG_SB)
    src = cr % HG_C
    rmask_f = own & (src <= tr)
    rmask_b = own & (src >= tr)
    side = {False: (qtf_ref, ktf_ref, khf_ref, rvf_ref, bef_ref),
            True: (qtb_ref, ktb_ref, khb_ref, rvb_ref, beb_ref)}

    _conv_gate(ca_ref, cb_ref, cc_ref, cw_ref, cbias_ref, ocv_ref)

    def chunk_rows(c):
        return pl.ds(pl.multiple_of(c * HG_C, HG_C), HG_C)

    def park(refs, slot, items):
        for idx, vals in enumerate(items):
            for ref, val in zip(refs, vals):
                ref[slot + idx] = val

    n_blk = nc // HG_UNROLL

    def pass1(i, carry):
        first = i * HG_UNROLL
        last = nc - HG_UNROLL - first
        decs = {False: jnp.exp(bef_ref[pl.ds(pl.multiple_of(first, HG_UNROLL), HG_UNROLL), :]),
                True: jnp.exp(beb_ref[pl.ds(pl.multiple_of(last, HG_UNROLL), HG_UNROLL), :])}
        steps = []
        for u in range(HG_UNROLL):
            for rev, c, row in ((False, first + u, u), (True, nc - 1 - first - u, HG_UNROLL - 1 - u)):
                upd = lax.dot_general(v_ref[chunk_rows(c), :], side[rev][2][chunk_rows(c), :], _TN,
                                      preferred_element_type=F32)
                steps.append((rev, c, decs[rev][row:row + 1, :], upd))
        st = {False: stf_ref[...], True: stb_ref[...]}
        for rev, c, dec, upd in steps:
            s_ref[c, :, (w if rev else 0):(2 * w if rev else w)] = st[rev].astype(BF16)
            st[rev] = st[rev] * dec + upd
        stf_ref[...] = st[False]
        stb_ref[...] = st[True]
        return carry

    stf_ref[...] = jnp.zeros_like(stf_ref)
    stb_ref[...] = jnp.zeros_like(stb_ref)
    lax.fori_loop(0, n_blk, pass1, 0)

    def scores(blk):
        pre = []
        blk_refs = pl.ds(pl.multiple_of(blk * (HG_UNROLL * HG_NSB), HG_UNROLL * HG_NSB),
                         HG_UNROLL * HG_NSB)
        rvs = {rev: side[rev][3][blk_refs, :] for rev in (False, True)}
        for u in range(HG_UNROLL):
            c = blk * HG_UNROLL + u
            rows = chunk_rows(c)
            rs, qhs = [], []
            for rev in (False, True):
                qt_ref, kt_ref = side[rev][:2]
                refs = [rvs[rev][HG_NSB * u + j:HG_NSB * u + j + 1, :] for j in range(HG_NSB)]
                qt = qt_ref[rows, :]
                kall = _key_variants(kt_ref[rows, :], refs, rev)
                rs.append(lax.dot_general(qt, kall, _NT, preferred_element_type=F32))
                qhs.append(qt.astype(F32) * _per_block([jnp.exp(r) for r in refs]))
            pre.append((c, rs, jnp.concatenate(qhs, axis=1).astype(BF16)))
        out = []
        for c, (r_f, r_b), qh in pre:
            oi = lax.dot_general(qh, s_ref[c], _NT, preferred_element_type=F32)
            rm = (jnp.where(rmask_f, r_f, 0.0) + jnp.where(rmask_b, r_b, 0.0)).astype(BF16)
            out.append((rm, oi))
        return out

    park((rm_ref, oi_ref), 0, scores(0))

    def outputs(i, slot):
        outs = []
        for u in range(HG_UNROLL):
            rows = chunk_rows(i * HG_UNROLL + u)
            v4 = jnp.concatenate([v_ref[rows, :]] * HG_NSB, axis=0)
            o = oi_ref[slot + u] + jnp.dot(rm_ref[slot + u], v4, preferred_element_type=F32)
            g = g_ref[rows, :].astype(F32)
            o = _rms_scale(o, gn_ref[...]) * (g * jax.nn.sigmoid(g))
            outs.append((rows, o.astype(BF16)))
        return outs

    def commit_outputs(outs):
        for rows, o in outs:
            o_ref[rows, :] = o

    def pass2(i, carry):
        cur = (i & 1) * HG_UNROLL
        outs = outputs(i, cur)
        fresh = scores(i + 1)
        commit_outputs(outs)
        park((rm_ref, oi_ref), HG_UNROLL - cur, fresh)
        return carry

    lax.fori_loop(0, n_blk - 1, pass2, 0)
    commit_outputs(outputs(n_blk - 1, ((n_blk - 1) & 1) * HG_UNROLL))


def _hgrn_conv(proj, ops, gnorm, conv_w, conv_b, layer, batch, t_len):
    assert (t_len // HG_C) % HG_UNROLL == 0 and MIX_W // CV_GROUP == HG_HEADS
    assert HG_UNROLL == SUBLANES
    nc = t_len // HG_C
    qt, kt, kh, rv, be = ops
    spec = lambda c0: pl.BlockSpec((t_len, LANES), lambda b, h: (b, c0 + h))
    both = lambda rows: [pl.BlockSpec((rows, LANES), lambda b, h: (b, h)),
                         pl.BlockSpec((rows, LANES), lambda b, h: (b, HG_HEADS + h))]
    par = lambda rows: pl.BlockSpec((None, rows, LANES), lambda b, h: (layer, 0, h))
    out = jax.ShapeDtypeStruct((batch * t_len, MIX_W), BF16)
    return pl.pallas_call(
        _hgrn_conv_body,
        grid=(batch, HG_HEADS),
        in_specs=[spec(_COL["hi"]), spec(_COL["hg"]), spec(_COL["ca"]), spec(_COL["cb"]),
                  spec(_COL["cc"])]
        + both(t_len) + both(t_len) + both(t_len) + both(t_len // HG_SB) + both(nc)
        + [par(1), par(3), par(1)],
        out_specs=[pl.BlockSpec((t_len, LANES), lambda b, h: (b, h))] * 2,
        out_shape=[out, out],
        scratch_shapes=[pltpu.VMEM((nc, HG_DK, 2 * HG_DK), BF16),
                        pltpu.VMEM((HG_DK, HG_DK), F32), pltpu.VMEM((HG_DK, HG_DK), F32),
                        pltpu.VMEM((2 * HG_UNROLL, HG_C, HG_NSB * HG_C), BF16),
                        pltpu.VMEM((2 * HG_UNROLL, HG_C, HG_DK), F32)],
        compiler_params=_cparams(("parallel", "parallel")),
    )(proj, proj, proj, proj, proj, qt, qt, kt, kt, kh, kh, rv, rv, be, be, gnorm, conv_w, conv_b)


def _na_bias_table(rpb):
    n_layers, n_dr = rpb.shape[0], rpb.shape[2]
    qc = np.arange(GRID_W)[:, None]
    kc = np.arange(GRID_W)[None, :]
    cs = np.clip(qc - NA_KW // 2, 0, GRID_W - NA_KW)
    col_ok = (kc >= cs) & (kc < cs + NA_KW)
    dc = np.clip(kc - qc + NA_KW - 1, 0, 2 * NA_KW - 2)
    pick = (dc[None] == np.arange(2 * NA_KW - 1)[:, None, None]) & col_ok[None]
    t = jnp.einsum("lhrc,cqk->lhrqk", rpb.astype(F32), jnp.asarray(pick, F32),
                   precision=lax.Precision.HIGHEST)
    t = t + jnp.asarray(np.where(col_ok, 0.0, MASK_VALUE), F32)
    t = t.reshape(n_layers, NA_HEADS // 2, 2, n_dr, GRID_W, GRID_W).transpose(0, 1, 3, 2, 4, 5)
    t = t.reshape(n_layers, NA_HEADS // 2, n_dr, 2 * GRID_W, GRID_W)
    return jnp.concatenate([t[:, :, :-1], t[:, :, 1:]], axis=-1)


def _na_body(q_ref, k_ref, v_ref, bias_ref, o_ref):
    n_rows = q_ref.shape[0] // GRID_W
    lane = lax.broadcasted_iota(jnp.int32, (GRID_W, LANES), 1)
    first = lane < NA_DH
    m0 = jnp.where(first, NA_SCALE, 0.0)
    m1 = jnp.where(first, 0.0, NA_SCALE)
    win = NA_KH * GRID_W

    def window(r):
        rs = jnp.clip(r - NA_KH // 2, 0, n_rows - NA_KH)
        return rs, pl.ds(pl.multiple_of(rs * GRID_W, GRID_W), win)

    def logits(r):
        rs, krows = window(r)
        q = q_ref[pl.ds(pl.multiple_of(r * GRID_W, GRID_W), GRID_W), :].astype(F32)
        q2 = jnp.concatenate([q * m0, q * m1], axis=0).astype(BF16)
        first_dr = NA_KH - 1 - (r - rs)
        bias = jnp.concatenate([bias_ref[first_dr + 2 * m] for m in range(NA_KH // 2)], axis=1)
        return lax.dot_general(q2, k_ref[krows, :], _NT, preferred_element_type=F32) + bias

    def attend(s, vw):
        p = jnp.exp(s - jnp.max(s, axis=-1, keepdims=True))
        l = jnp.sum(p, axis=-1, keepdims=True)
        o2 = jnp.dot(p.astype(BF16), vw, preferred_element_type=F32) / l
        return jnp.where(first, o2[:GRID_W, :], o2[GRID_W:, :]).astype(BF16)

    def rows_step(i, carry):
        rows = [i * NA_UNROLL + u for u in range(NA_UNROLL)]
        scores = [logits(r) for r in rows]
        outs = [attend(s, v_ref[window(r)[1], :]) for s, r in zip(scores, rows)]
        for r, o in zip(rows, outs):
            o_ref[pl.ds(pl.multiple_of(r * GRID_W, GRID_W), GRID_W), :] = o
        return carry

    lax.fori_loop(0, n_rows // NA_UNROLL, rows_step, 0)


def _natten(proj, bias, layer, batch, t_len):
    assert t_len % GRID_W == 0 and t_len // GRID_W >= NA_KH
    assert (t_len // GRID_W) % NA_UNROLL == 0
    spec = lambda c0: pl.BlockSpec((t_len, LANES), lambda p, b: (b, c0 + p))
    return pl.pallas_call(
        _na_body,
        grid=(NA_HEADS // 2, batch),
        in_specs=[spec(_COL["nq"]), spec(_COL["nk"]), spec(_COL["nv"]),
                  pl.BlockSpec((None, None, 2 * NA_KH - 2, 2 * GRID_W, 2 * GRID_W),
                               lambda p, b: (layer, p, 0, 0, 0))],
        out_specs=pl.BlockSpec((t_len, LANES), lambda p, b: (b, p)),
        out_shape=jax.ShapeDtypeStruct((batch * t_len, MIX_W), BF16),
        compiler_params=_cparams(("parallel", "parallel")),
    )(proj, proj, proj, bias)


def _mix_ffn_body(x_ref, ghg_ref, gcv_ref, gna_ref, ohg_ref, ocv_ref, ona_ref,
                  whg_ref, wcv_ref, wna_ref, wout_ref, g_ref, wg_ref, wu_ref, wd_ref, *rest, final):
    o_ref = rest[-1]
    ys = [jnp.dot(a_ref[...], w_ref[...], preferred_element_type=F32)
          for a_ref, w_ref in ((ohg_ref, whg_ref), (ocv_ref, wcv_ref), (ona_ref, wna_ref))]
    m = sum(jax.nn.sigmoid(g_ref[...].astype(F32)) * y
            for g_ref, y in zip((ghg_ref, gcv_ref, gna_ref), ys))
    x = x_ref[...] + jnp.dot(m.astype(BF16), wout_ref[...], preferred_element_type=F32)
    y = _half_swiglu(x, g_ref, wg_ref, wu_ref, wd_ref)
    if final:
        y = _rms_scale(y, rest[0][...])
    o_ref[...] = y


def _mix_ffn(x, proj, o_hg, o_cv, o_na, w_hg, w_cv, w_na, w_out, gain, w_gu, w_down, layer,
             final_gain=None):
    n, d = x.shape
    final = final_gain is not None
    gate = lambda c0: pl.BlockSpec((TM, d), lambda i: (i, c0 * LANES // d))
    act = pl.BlockSpec((TM, MIX_W), lambda i: (i, 0))
    wmix = _resident((None, MIX_W, d), lambda i: (layer, 0, 0))
    in_specs = [pl.BlockSpec((TM, d), lambda i: (i, 0)),
                gate(_COL["g_hg"]), gate(_COL["g_cv"]), gate(_COL["g_na"]),
                act, act, act, wmix, wmix, wmix,
                _resident((None, d, d), lambda i: (layer, 0, 0))]
    in_specs += _ffn_weight_specs(d, w_down.shape[1], layer)
    args = [x, proj, proj, proj, o_hg, o_cv, o_na, w_hg, w_cv, w_na, w_out,
            gain, w_gu, w_gu, w_down]
    if final:
        in_specs.append(_resident((1, d), lambda i: (0, 0)))
        args.append(final_gain)
    return pl.pallas_call(
        functools.partial(_mix_ffn_body, final=final),
        grid=(n // TM,),
        in_specs=in_specs,
        out_specs=pl.BlockSpec((TM, d), lambda i: (i, 0)),
        out_shape=jax.ShapeDtypeStruct((n, d), F32),
        compiler_params=_cparams(("parallel",)),
    )(*args)


def _lower_bounds(lb_logits):
    p = jax.nn.softmax(lb_logits.astype(F32), axis=1)
    return jnp.cumsum(p, axis=1) - p[:, :1]


def _trunk(x3, p):
    batch, t_len, d = x3.shape
    assert d % LANES == 0 and t_len % HG_C == 0 and (batch * t_len) % TM == 0
    n_layers = p["n_layers"]
    x = x3.reshape(batch * t_len, d)
    for l in range(n_layers):
        x = _ffn(x, p["ffn1_norm"], p["ffn1_w_gu"], p["ffn1_w_down"], l)
        proj, *hg_ops = _inproj(x, p["mix_norm"], p["w_in"], p["lb"], p["run_sums"], l)
        o_hg, o_cv = _hgrn_conv(proj, hg_ops, p["hg_out_norm"], p["conv_w"], p["conv_b"],
                                l, batch, t_len)
        o_na = _natten(proj, p["na_bias"], l, batch, t_len)
        x = _mix_ffn(x, proj, o_hg, o_cv, o_na, p["w_hg_out"], p["w_cv_out"], p["w_na_out"],
                     p["w_out"], p["ffn2_norm"], p["ffn2_w_gu"], p["ffn2_w_down"], l,
                     final_gain=p["final_norm"] if l == n_layers - 1 else None)
    return x.reshape(batch, t_len, d)


def kernel(x_prompt, x_sample, ffn1_norm, ffn1_w_gu, ffn1_w_down, mix_norm, w_in, hg_lb_logits, hg_out_norm, w_hg_out, conv_w, conv_b, w_cv_out, na_rpb, w_na_out, w_out, ffn2_norm, ffn2_w_gu, ffn2_w_down, final_norm):
    n_layers, d, n_in = w_in.shape
    assert n_in == N_IN
    row = lambda a: a.astype(F32)[:, None, :]
    p = dict(
        n_layers=n_layers,
        ffn1_norm=row(ffn1_norm), ffn1_w_gu=ffn1_w_gu.astype(BF16), ffn1_w_down=ffn1_w_down.astype(BF16),
        ffn2_norm=row(ffn2_norm), ffn2_w_gu=ffn2_w_gu.astype(BF16), ffn2_w_down=ffn2_w_down.astype(BF16),
        mix_norm=row(mix_norm),
        w_in=jnp.concatenate([w_in[..., _SRC[m][0] * LANES:_SRC[m][1] * LANES] for m in _ORDER],
                             axis=-1).astype(BF16),
        lb=_lower_bounds(hg_lb_logits).transpose(1, 0, 2),
        run_sums=_running_sum_matrix(),
        hg_out_norm=row(hg_out_norm),
        w_hg_out=w_hg_out.astype(BF16), w_cv_out=w_cv_out.astype(BF16), w_na_out=w_na_out.astype(BF16),
        conv_w=conv_w.astype(F32), conv_b=row(conv_b),
        na_bias=_na_bias_table(na_rpb),
        w_out=w_out.astype(BF16),
        final_norm=final_norm.astype(F32)[None, :],
    )
    return _trunk(x_prompt, p), _trunk(x_sample, p)
```

```python
import functools

import numpy as np
import jax
import jax.numpy as jnp
from jax import lax
from jax.experimental import pallas as pl
from jax.experimental.pallas import tpu as pltpu

F32 = jnp.float32
BF16 = jnp.bfloat16

GRID_W = 64
RMS_EPS = 1e-6
F_FLOOR = 1e-30
MASK_VALUE = -1e30
HG_HEADS, HG_DK = 4, 128
CV_GROUP = 128
NA_HEADS, NA_DH = 8, 64
NA_KH, NA_KW = 8, 16
NA_SCALE = NA_DH ** -0.5
MIX_W = HG_HEADS * HG_DK

LANES = 128
SUBLANES = 8
MXU_N = 256
VMEM_LIMIT = 56 * 1024 * 1024

TM = 512
TM_FFN = 1024
N_CHUNK = 4 * MXU_N
HG_C = 64
HG_SB = 16
HG_NSB = HG_C // HG_SB
HG_UNROLL = 8
HG_UNROLL2 = 16
NA_UNROLL = 32

_SRC = dict(hq=(0, 4), hi=(4, 8), hzf=(8, 12), hzb=(12, 16), hg=(16, 20), ca=(20, 24), cb=(24, 28),
            cc=(28, 32), nq=(32, 36), nk=(36, 40), nv=(40, 44), g_hg=(44, 52), g_cv=(52, 60),
            g_na=(60, 68))
_ORDER = ("g_hg", "g_cv", "g_na", "hi", "hg", "ca", "cb", "cc", "nq", "nk", "nv", "hq", "hzf", "hzb")
_COL = {}
for _name in _ORDER:
    _COL[_name] = sum(_SRC[m][1] - _SRC[m][0] for m in _ORDER[:_ORDER.index(_name)])
N_IN = 68 * LANES
N_STORE = _COL["hq"] * LANES

_NT = (((1,), (1,)), ((), ()))
_TN = (((0,), (0,)), ((), ()))


def _cparams(sem):
    return pltpu.CompilerParams(dimension_semantics=sem, vmem_limit_bytes=VMEM_LIMIT)


def _resident(block_shape, index_map):
    return pl.BlockSpec(block_shape, index_map, pipeline_mode=pl.Buffered(1))


def _rms_scale(x, gain):
    return x * lax.rsqrt(jnp.mean(x * x, axis=-1, keepdims=True) + RMS_EPS) * gain


def _col_chunks(n):
    return [(c, min(c + N_CHUNK, n)) for c in range(0, n, N_CHUNK)]


def _half_swiglu(x, g_ref, wg_ref, wu_ref, wd_ref):
    xn = _rms_scale(x, g_ref[...]).astype(BF16)

    def hidden(c0, c1):
        a = jnp.dot(xn, wg_ref[:, c0:c1], preferred_element_type=F32)
        b = jnp.dot(xn, wu_ref[:, c0:c1], preferred_element_type=F32)
        return (a * jax.nn.sigmoid(a) * b).astype(BF16)

    chunks = _col_chunks(wd_ref.shape[0])
    acc = None
    h = hidden(*chunks[0])
    for idx, (c0, c1) in enumerate(chunks):
        h_next = hidden(*chunks[idx + 1]) if idx + 1 < len(chunks) else None
        part = jnp.dot(h, wd_ref[c0:c1, :], preferred_element_type=F32)
        acc = part if acc is None else acc + part
        h = h_next
    return x + 0.5 * acc


def _ffn_weight_specs(d, d_ff, layer):
    return [_resident((None, 1, d), lambda i: (layer, 0, 0)),
            _resident((None, d, d_ff), lambda i: (layer, 0, 0)),
            _resident((None, d, d_ff), lambda i: (layer, 0, 1)),
            _resident((None, d_ff, d), lambda i: (layer, 0, 0))]


def _ffn_body(x_ref, g_ref, wg_ref, wu_ref, wd_ref, o_ref):
    o_ref[...] = _half_swiglu(x_ref[...], g_ref, wg_ref, wu_ref, wd_ref)


def _ffn(x, gain, w_gu, w_down, layer):
    n, d = x.shape
    assert n % TM_FFN == 0
    return pl.pallas_call(
        _ffn_body,
        grid=(n // TM_FFN,),
        in_specs=[pl.BlockSpec((TM_FFN, d), lambda i: (i, 0))]
        + _ffn_weight_specs(d, w_down.shape[1], layer),
        out_specs=pl.BlockSpec((TM_FFN, d), lambda i: (i, 0)),
        out_shape=jax.ShapeDtypeStruct((n, d), F32),
        compiler_params=_cparams(("parallel",)),
    )(x, gain, w_gu, w_gu, w_down)


def _running_sum_matrix():
    tok = np.arange(TM)
    chunk, pos = tok // HG_C, tok % HG_C
    ref_tok = np.arange(TM // HG_SB) * HG_SB + HG_SB // 2
    mats = []
    for rev in (False, True):
        end_tok = np.arange(TM // HG_C) * HG_C + (0 if rev else HG_C - 1)
        at = np.concatenate([tok, ref_tok, end_tok])
        same = chunk[None, :] == chunk[at][:, None]
        upto = pos[None, :] >= pos[at][:, None] if rev else pos[None, :] <= pos[at][:, None]
        pad = np.zeros((-len(at) % (2 * SUBLANES), TM), bool)
        mats.append(np.concatenate([same & upto, pad], axis=0))
    return jnp.asarray(np.stack(mats).astype(np.float32), BF16)


def _inproj_body(x_ref, g_ref, w_ref, lb_ref, sum_ref,
                 o_ref, qt_ref, kt_ref, kh_ref, rv_ref, be_ref):
    xn = _rms_scale(x_ref[...], g_ref[...]).astype(BF16)
    n_store = o_ref.shape[1]
    w = kt_ref.shape[1] // 2

    def proj(c0, c1):
        return jnp.dot(xn, w_ref[:, c0:c1], preferred_element_type=F32)

    q = proj(n_store, n_store + w)
    gated = []
    for side in range(2):
        z = proj(n_store + (1 + side) * w, n_store + (2 + side) * w)
        lb = lb_ref[side:side + 1, :]
        f = lb + (1.0 - lb) * jax.nn.sigmoid(z)
        gated.append((1.0 - f, jnp.log(jnp.maximum(f, F_FLOOR)).astype(BF16)))
    pieces = [(side, c0) for side in range(2) for c0 in range(0, w, MXU_N)]
    slab = _col_chunks(n_store)
    every = len(slab) // (len(pieces) + 1)
    sums = {}
    for idx, (c0, c1) in enumerate(slab):
        o_ref[:, c0:c1] = proj(c0, c1).astype(BF16)
        if (idx + 1) % every == 0 and len(sums) < len(pieces):
            side, p0 = pieces[len(sums)]
            logf = gated[side][1][:, p0:p0 + MXU_N]
            parts = [jnp.dot(sum_ref[side, r0:r0 + LANES, r0:r0 + LANES], logf[r0:r0 + LANES],
                             preferred_element_type=F32) for r0 in range(0, TM, LANES)]
            parts.append(jnp.dot(sum_ref[side, TM:, :], logf, preferred_element_type=F32))
            sums[side, p0] = jnp.concatenate(parts, axis=0)
    n_r, n_e = TM // HG_SB, TM // HG_C
    for (side, p0), full in sums.items():
        k = gated[side][0][:, p0:p0 + MXU_N]
        b = full[:TM]
        r_rows, e_rows = [], []
        for r0 in range(0, TM, HG_C):
            e = b[r0:r0 + 1, :] if side else b[r0 + HG_C - 1:r0 + HG_C, :]
            e_rows.append(jnp.broadcast_to(e, (HG_C, MXU_N)))
            for s0 in range(r0, r0 + HG_C, HG_SB):
                r = b[s0 + HG_SB // 2:s0 + HG_SB // 2 + 1, :]
                r_rows.append(jnp.broadcast_to(r, (HG_SB, MXU_N)))
        r_rows = jnp.concatenate(r_rows, axis=0)
        e_rows = jnp.concatenate(e_rows, axis=0)
        d = r_rows - b
        cols = slice(side * w + p0, side * w + p0 + MXU_N)
        kt_ref[:, cols] = (k * jnp.exp(d)).astype(BF16)
        qt_ref[:, cols] = (q[:, p0:p0 + MXU_N] * jnp.exp(-d)).astype(BF16)
        kh_ref[:, cols] = (k * jnp.exp(e_rows - b)).astype(BF16)
        rv_ref[:, cols] = full[TM:TM + n_r]
        be_ref[:, cols] = full[TM + n_r:TM + n_r + n_e]


def _inproj(x, gain, w_in, lb, sums, layer):
    n, d = x.shape
    n_in = w_in.shape[-1]
    w2 = 2 * MIX_W
    rows = lambda r: pl.BlockSpec((r, w2), lambda i: (i, 0))
    return pl.pallas_call(
        _inproj_body,
        grid=(n // TM,),
        in_specs=[
            pl.BlockSpec((TM, d), lambda i: (i, 0)),
            _resident((None, 1, d), lambda i: (layer, 0, 0)),
            _resident((None, d, n_in), lambda i: (layer, 0, 0)),
            _resident((None, 2, MIX_W), lambda i: (layer, 0, 0)),
            _resident(sums.shape, lambda i: (0, 0, 0)),
        ],
        out_specs=[pl.BlockSpec((TM, N_STORE), lambda i: (i, 0)), rows(TM), rows(TM), rows(TM),
                   rows(TM // HG_SB), rows(TM // HG_C)],
        out_shape=[jax.ShapeDtypeStruct((n, N_STORE), BF16)]
        + [jax.ShapeDtypeStruct((n, w2), BF16)] * 3
        + [jax.ShapeDtypeStruct((n // HG_SB, w2), F32),
           jax.ShapeDtypeStruct((n // HG_C, w2), F32)],
        compiler_params=_cparams(("parallel",)),
    )(x, gain, w_in, lb, sums)


def _per_block(vecs):
    return jnp.concatenate([jnp.broadcast_to(v, (HG_SB, v.shape[1])) for v in vecs], axis=0)


def _key_variants(kt, refs, rev):
    blocks = [kt[HG_SB * j:HG_SB * (j + 1), :] for j in range(HG_NSB)]
    wide = [blk.astype(F32) for blk in blocks]
    zero = jnp.zeros(blocks[0].shape, BF16)
    parts = []
    for i in range(HG_NSB):
        for j in range(HG_NSB):
            if j == i:
                parts.append(blocks[j])
            elif (j > i) if rev else (j < i):
                parts.append((wide[j] * jnp.exp(refs[i] - refs[j])).astype(BF16))
            else:
                parts.append(zero)
    return jnp.concatenate(parts, axis=0)


def _conv_gate(ca_ref, cb_ref, cc_ref, w_ref, bias_ref, o_ref):
    t_len = ca_ref.shape[0]
    z = cc_ref[...].astype(F32) * ca_ref[...].astype(F32)
    edge = lax.broadcasted_iota(jnp.int32, (SUBLANES, z.shape[1]), 0)
    prev = pltpu.roll(z, 1, axis=0)
    prev = jnp.concatenate([jnp.where(edge == 0, 0.0, prev[:SUBLANES]), prev[SUBLANES:]], axis=0)
    nxt = pltpu.roll(z, t_len - 1, axis=0)
    nxt = jnp.concatenate([nxt[:-SUBLANES],
                           jnp.where(edge == SUBLANES - 1, 0.0, nxt[-SUBLANES:])], axis=0)
    zc = prev * w_ref[0:1, :] + z * w_ref[1:2, :] + nxt * w_ref[2:3, :]
    o_ref[...] = (cb_ref[...].astype(F32) * (zc + bias_ref[...])).astype(BF16)


def _hgrn_conv_body(v_ref, g_ref, ca_ref, cb_ref, cc_ref,
                    qtf_ref, qtb_ref, ktf_ref, ktb_ref, khf_ref, khb_ref,
                    rvf_ref, rvb_ref, bef_ref, beb_ref, gn_ref, cw_ref, cbias_ref,
                    o_ref, ocv_ref, s_ref, stf_ref, stb_ref, rm_ref, oi_ref):
    t_len = v_ref.shape[0]
    nc = t_len // HG_C
    w = HG_DK
    tr = lax.broadcasted_iota(jnp.int32, (HG_C, HG_NSB * HG_C), 0)
    cr = lax.broadcasted_iota(jnp.int32, (HG_C, HG_NSB * HG_C), 1)
    own = (cr // HG_C) == (tr // HG_SB)
    src = cr % HG_C
    rmask_f = own & (src <= tr)
    rmask_b = own & (src >= tr)
    side = {False: (qtf_ref, ktf_ref, khf_ref, rvf_ref, bef_ref),
            True: (qtb_ref, ktb_ref, khb_ref, rvb_ref, beb_ref)}

    _conv_gate(ca_ref, cb_ref, cc_ref, cw_ref, cbias_ref, ocv_ref)

    def chunk_rows(c):
        return pl.ds(pl.multiple_of(c * HG_C, HG_C), HG_C)

    def park(refs, slot, items):
        for idx, vals in enumerate(items):
            for ref, val in zip(refs, vals):
                ref[slot + idx] = val

    n_blk = nc // HG_UNROLL

    def pass1(i, carry):
        first = i * HG_UNROLL
        last = nc - HG_UNROLL - first
        decs = {False: jnp.exp(bef_ref[pl.ds(pl.multiple_of(first, HG_UNROLL), HG_UNROLL), :]),
                True: jnp.exp(beb_ref[pl.ds(pl.multiple_of(last, HG_UNROLL), HG_UNROLL), :])}
        steps = []
        for u in range(HG_UNROLL):
            for rev, c, row in ((False, first + u, u), (True, nc - 1 - first - u, HG_UNROLL - 1 - u)):
                upd = lax.dot_general(v_ref[chunk_rows(c), :], side[rev][2][chunk_rows(c), :], _TN,
                                      preferred_element_type=F32)
                steps.append((rev, c, decs[rev][row:row + 1, :], upd))
        st = {False: stf_ref[...], True: stb_ref[...]}
        for rev, c, dec, upd in steps:
            s_ref[c, :, (w if rev else 0):(2 * w if rev else w)] = st[rev].astype(BF16)
            st[rev] = st[rev] * dec + upd
        stf_ref[...] = st[False]
        stb_ref[...] = st[True]
        return carry

    stf_ref[...] = jnp.zeros_like(stf_ref)
    stb_ref[...] = jnp.zeros_like(stb_ref)
    lax.fori_loop(0, n_blk, pass1, 0)

    n_blk2 = nc // HG_UNROLL2

    def scores(blk):
        pre = []
        blk_refs = pl.ds(pl.multiple_of(blk * (HG_UNROLL2 * HG_NSB), HG_UNROLL2 * HG_NSB),
                         HG_UNROLL2 * HG_NSB)
        rvs = {rev: side[rev][3][blk_refs, :] for rev in (False, True)}
        for u in range(HG_UNROLL2):
            c = blk * HG_UNROLL2 + u
            rows = chunk_rows(c)
            rs, qhs = [], []
            for rev in (False, True):
                qt_ref, kt_ref = side[rev][:2]
                refs = [rvs[rev][HG_NSB * u + j:HG_NSB * u + j + 1, :] for j in range(HG_NSB)]
                qt = qt_ref[rows, :]
                kall = _key_variants(kt_ref[rows, :], refs, rev)
                rs.append(lax.dot_general(qt, kall, _NT, preferred_element_type=F32))
                qhs.append(qt.astype(F32) * _per_block([jnp.exp(r) for r in refs]))
            pre.append((c, rs, jnp.concatenate(qhs, axis=1).astype(BF16)))
        out = []
        for c, (r_f, r_b), qh in pre:
            oi = lax.dot_general(qh, s_ref[c], _NT, preferred_element_type=F32)
            rm = (jnp.where(rmask_f, r_f, 0.0) + jnp.where(rmask_b, r_b, 0.0)).astype(BF16)
            out.append((rm, oi))
        return out

    park((rm_ref, oi_ref), 0, scores(0))

    def outputs(i, slot):
        outs = []
        for u in range(HG_UNROLL2):
            rows = chunk_rows(i * HG_UNROLL2 + u)
            v4 = jnp.concatenate([v_ref[rows, :]] * HG_NSB, axis=0)
            o = oi_ref[slot + u] + jnp.dot(rm_ref[slot + u], v4, preferred_element_type=F32)
            g = g_ref[rows, :].astype(F32)
            o = _rms_scale(o, gn_ref[...]) * (g * jax.nn.sigmoid(g))
            outs.append((rows, o.astype(BF16)))
        return outs

    def commit_outputs(outs):
        for rows, o in outs:
            o_ref[rows, :] = o

    def pass2(i, carry):
        cur = (i & 1) * HG_UNROLL2
        outs = outputs(i, cur)
        fresh = scores(i + 1)
        commit_outputs(outs)
        park((rm_ref, oi_ref), HG_UNROLL2 - cur, fresh)
        return carry

    lax.fori_loop(0, n_blk2 - 1, pass2, 0)
    commit_outputs(outputs(n_blk2 - 1, ((n_blk2 - 1) & 1) * HG_UNROLL2))


def _hgrn_conv(proj, ops, gnorm, conv_w, conv_b, layer, batch, t_len):
    assert (t_len // HG_C) % HG_UNROLL2 == 0 and HG_UNROLL2 % HG_UNROLL == 0
    assert MIX_W // CV_GROUP == HG_HEADS
    assert HG_UNROLL == SUBLANES
    nc = t_len // HG_C
    qt, kt, kh, rv, be = ops
    spec = lambda c0: pl.BlockSpec((t_len, LANES), lambda b, h: (b, c0 + h))
    both = lambda rows: [pl.BlockSpec((rows, LANES), lambda b, h: (b, h)),
                         pl.BlockSpec((rows, LANES), lambda b, h: (b, HG_HEADS + h))]
    par = lambda rows: pl.BlockSpec((None, rows, LANES), lambda b, h: (layer, 0, h))
    out = jax.ShapeDtypeStruct((batch * t_len, MIX_W), BF16)
    return pl.pallas_call(
        _hgrn_conv_body,
        grid=(batch, HG_HEADS),
        in_specs=[spec(_COL["hi"]), spec(_COL["hg"]), spec(_COL["ca"]), spec(_COL["cb"]),
                  spec(_COL["cc"])]
        + both(t_len) + both(t_len) + both(t_len) + both(t_len // HG_SB) + both(nc)
        + [par(1), par(3), par(1)],
        out_specs=[pl.BlockSpec((t_len, LANES), lambda b, h: (b, h))] * 2,
        out_shape=[out, out],
        scratch_shapes=[pltpu.VMEM((nc, HG_DK, 2 * HG_DK), BF16),
                        pltpu.VMEM((HG_DK, HG_DK), F32), pltpu.VMEM((HG_DK, HG_DK), F32),
                        pltpu.VMEM((2 * HG_UNROLL2, HG_C, HG_NSB * HG_C), BF16),
                        pltpu.VMEM((2 * HG_UNROLL2, HG_C, HG_DK), F32)],
        compiler_params=_cparams(("parallel", "parallel")),
    )(proj, proj, proj, proj, proj, qt, qt, kt, kt, kh, kh, rv, rv, be, be, gnorm, conv_w, conv_b)


def _na_bias_table(rpb):
    n_layers, n_dr = rpb.shape[0], rpb.shape[2]
    qc = np.arange(GRID_W)[:, None]
    kc = np.arange(GRID_W)[None, :]
    cs = np.clip(qc - NA_KW // 2, 0, GRID_W - NA_KW)
    col_ok = (kc >= cs) & (kc < cs + NA_KW)
    dc = np.clip(kc - qc + NA_KW - 1, 0, 2 * NA_KW - 2)
    pick = (dc[None] == np.arange(2 * NA_KW - 1)[:, None, None]) & col_ok[None]
    t = jnp.einsum("lhrc,cqk->lhrqk", rpb.astype(F32), jnp.asarray(pick, F32),
                   precision=lax.Precision.HIGHEST)
    t = t + jnp.asarray(np.where(col_ok, 0.0, MASK_VALUE), F32)
    t = t.reshape(n_layers, NA_HEADS // 2, 2, n_dr, GRID_W, GRID_W).transpose(0, 1, 3, 2, 4, 5)
    t = t.reshape(n_layers, NA_HEADS // 2, n_dr, 2 * GRID_W, GRID_W)
    return jnp.concatenate([t[:, :, :-1], t[:, :, 1:]], axis=-1)


def _na_body(q_ref, k_ref, v_ref, bias_ref, o_ref):
    n_rows = q_ref.shape[0] // GRID_W
    lane = lax.broadcasted_iota(jnp.int32, (GRID_W, LANES), 1)
    first = lane < NA_DH
    m0 = jnp.where(first, NA_SCALE, 0.0)
    m1 = jnp.where(first, 0.0, NA_SCALE)
    win = NA_KH * GRID_W

    def window(r):
        rs = jnp.clip(r - NA_KH // 2, 0, n_rows - NA_KH)
        return rs, pl.ds(pl.multiple_of(rs * GRID_W, GRID_W), win)

    def logits(r):
        rs, krows = window(r)
        q = q_ref[pl.ds(pl.multiple_of(r * GRID_W, GRID_W), GRID_W), :].astype(F32)
        q2 = jnp.concatenate([q * m0, q * m1], axis=0).astype(BF16)
        first_dr = NA_KH - 1 - (r - rs)
        bias = jnp.concatenate([bias_ref[first_dr + 2 * m] for m in range(NA_KH // 2)], axis=1)
        return lax.dot_general(q2, k_ref[krows, :], _NT, preferred_element_type=F32) + bias

    def attend(s, vw):
        p = jnp.exp(s - jnp.max(s, axis=-1, keepdims=True))
        l = jnp.sum(p, axis=-1, keepdims=True)
        o2 = jnp.dot(p.astype(BF16), vw, preferred_element_type=F32) / l
        return jnp.where(first, o2[:GRID_W, :], o2[GRID_W:, :]).astype(BF16)

    def rows_step(i, carry):
        rows = [i * NA_UNROLL + u for u in range(NA_UNROLL)]
        scores = [logits(r) for r in rows]
        outs = [attend(s, v_ref[window(r)[1], :]) for s, r in zip(scores, rows)]
        for r, o in zip(rows, outs):
            o_ref[pl.ds(pl.multiple_of(r * GRID_W, GRID_W), GRID_W), :] = o
        return carry

    lax.fori_loop(0, n_rows // NA_UNROLL, rows_step, 0)


def _natten(proj, bias, layer, batch, t_len):
    assert t_len % GRID_W == 0 and t_len // GRID_W >= NA_KH
    assert (t_len // GRID_W) % NA_UNROLL == 0
    spec = lambda c0: pl.BlockSpec((t_len, LANES), lambda p, b: (b, c0 + p))
    return pl.pallas_call(
        _na_body,
        grid=(NA_HEADS // 2, batch),
        in_specs=[spec(_COL["nq"]), spec(_COL["nk"]), spec(_COL["nv"]),
                  pl.BlockSpec((None, None, 2 * NA_KH - 2, 2 * GRID_W, 2 * GRID_W),
                               lambda p, b: (layer, p, 0, 0, 0))],
        out_specs=pl.BlockSpec((t_len, LANES), lambda p, b: (b, p)),
        out_shape=jax.ShapeDtypeStruct((batch * t_len, MIX_W), BF16),
        compiler_params=_cparams(("parallel", "parallel")),
    )(proj, proj, proj, bias)


def _mix_ffn_body(x_ref, ghg_ref, gcv_ref, gna_ref, ohg_ref, ocv_ref, ona_ref,
                  whg_ref, wcv_ref, wna_ref, wout_ref, g_ref, wg_ref, wu_ref, wd_ref, *rest, final):
    o_ref = rest[-1]
    ys = [jnp.dot(a_ref[...], w_ref[...], preferred_element_type=F32)
          for a_ref, w_ref in ((ohg_ref, whg_ref), (ocv_ref, wcv_ref), (ona_ref, wna_ref))]
    m = sum(jax.nn.sigmoid(g_ref[...].astype(F32)) * y
            for g_ref, y in zip((ghg_ref, gcv_ref, gna_ref), ys))
    x = x_ref[...] + jnp.dot(m.astype(BF16), wout_ref[...], preferred_element_type=F32)
    y = _half_swiglu(x, g_ref, wg_ref, wu_ref, wd_ref)
    if final:
        y = _rms_scale(y, rest[0][...])
    o_ref[...] = y


def _mix_ffn(x, proj, o_hg, o_cv, o_na, w_hg, w_cv, w_na, w_out, gain, w_gu, w_down, layer,
             final_gain=None):
    n, d = x.shape
    final = final_gain is not None
    gate = lambda c0: pl.BlockSpec((TM, d), lambda i: (i, c0 * LANES // d))
    act = pl.BlockSpec((TM, MIX_W), lambda i: (i, 0))
    wmix = _resident((None, MIX_W, d), lambda i: (layer, 0, 0))
    in_specs = [pl.BlockSpec((TM, d), lambda i: (i, 0)),
                gate(_COL["g_hg"]), gate(_COL["g_cv"]), gate(_COL["g_na"]),
                act, act, act, wmix, wmix, wmix,
                _resident((None, d, d), lambda i: (layer, 0, 0))]
    in_specs += _ffn_weight_specs(d, w_down.shape[1], layer)
    args = [x, proj, proj, proj, o_hg, o_cv, o_na, w_hg, w_cv, w_na, w_out,
            gain, w_gu, w_gu, w_down]
    if final:
        in_specs.append(_resident((1, d), lambda i: (0, 0)))
        args.append(final_gain)
    return pl.pallas_call(
        functools.partial(_mix_ffn_body, final=final),
        grid=(n // TM,),
        in_specs=in_specs,
        out_specs=pl.BlockSpec((TM, d), lambda i: (i, 0)),
        out_shape=jax.ShapeDtypeStruct((n, d), F32),
        compiler_params=_cparams(("parallel",)),
    )(*args)


def _lower_bounds(lb_logits):
    p = jax.nn.softmax(lb_logits.astype(F32), axis=1)
    return jnp.cumsum(p, axis=1) - p[:, :1]


def _trunk(x3, p):
    batch, t_len, d = x3.shape
    assert d % LANES == 0 and t_len % HG_C == 0 and (batch * t_len) % TM == 0
    n_layers = p["n_layers"]
    x = x3.reshape(batch * t_len, d)
    for l in range(n_layers):
        x = _ffn(x, p["ffn1_norm"], p["ffn1_w_gu"], p["ffn1_w_down"], l)
        proj, *hg_ops = _inproj(x, p["mix_norm"], p["w_in"], p["lb"], p["run_sums"], l)
        o_hg, o_cv = _hgrn_conv(proj, hg_ops, p["hg_out_norm"], p["conv_w"], p["conv_b"],
                                l, batch, t_len)
        o_na = _natten(proj, p["na_bias"], l, batch, t_len)
        x = _mix_ffn(x, proj, o_hg, o_cv, o_na, p["w_hg_out"], p["w_cv_out"], p["w_na_out"],
                     p["w_out"], p["ffn2_norm"], p["ffn2_w_gu"], p["ffn2_w_down"], l,
                     final_gain=p["final_norm"] if l == n_layers - 1 else None)
    return x.reshape(batch, t_len, d)


def kernel(x_prompt, x_sample, ffn1_norm, ffn1_w_gu, ffn1_w_down, mix_norm, w_in, hg_lb_logits, hg_out_norm, w_hg_out, conv_w, conv_b, w_cv_out, na_rpb, w_na_out, w_out, ffn2_norm, ffn2_w_gu, ffn2_w_down, final_norm):
    n_layers, d, n_in = w_in.shape
    assert n_in == N_IN
    row = lambda a: a.astype(F32)[:, None, :]
    p = dict(
        n_layers=n_layers,
        ffn1_norm=row(ffn1_norm), ffn1_w_gu=ffn1_w_gu.astype(BF16), ffn1_w_down=ffn1_w_down.astype(BF16),
        ffn2_norm=row(ffn2_norm), ffn2_w_gu=ffn2_w_gu.astype(BF16), ffn2_w_down=ffn2_w_down.astype(BF16),
        mix_norm=row(mix_norm),
        w_in=jnp.concatenate([w_in[..., _SRC[m][0] * LANES:_SRC[m][1] * LANES] for m in _ORDER],
                             axis=-1).astype(BF16),
        lb=_lower_bounds(hg_lb_logits).transpose(1, 0, 2),
        run_sums=_running_sum_matrix(),
        hg_out_norm=row(hg_out_norm),
        w_hg_out=w_hg_out.astype(BF16), w_cv_out=w_cv_out.astype(BF16), w_na_out=w_na_out.astype(BF16),
        conv_w=conv_w.astype(F32), conv_b=row(conv_b),
        na_bias=_na_bias_table(na_rpb),
        w_out=w_out.astype(BF16),
        final_norm=final_norm.astype(F32)[None, :],
    )
    return _trunk(x_prompt, p), _trunk(x_sample, p)
```

```python
import functools

import numpy as np
import jax
import jax.numpy as jnp
from jax import lax
from jax.experimental import pallas as pl
from jax.experimental.pallas import tpu as pltpu

F32 = jnp.float32
BF16 = jnp.bfloat16

GRID_W = 64
RMS_EPS = 1e-6
F_FLOOR = 1e-30
MASK_VALUE = -1e30
HG_HEADS, HG_DK = 4, 128
CV_GROUP = 128
NA_HEADS, NA_DH = 8, 64
NA_KH, NA_KW = 8, 16
NA_SCALE = NA_DH ** -0.5
MIX_W = HG_HEADS * HG_DK

LANES = 128
SUBLANES = 8
MXU_N = 256
VMEM_LIMIT = 56 * 1024 * 1024

TM = 512
TM_FFN = 1024
N_CHUNK = 4 * MXU_N
HG_C = 64
HG_SB = 16
HG_NSB = HG_C // HG_SB
HG_UNROLL = 16
HG_UNROLL2 = 32
NA_UNROLL = 32

_SRC = dict(hq=(0, 4), hi=(4, 8), hzf=(8, 12), hzb=(12, 16), hg=(16, 20), ca=(20, 24), cb=(24, 28),
            cc=(28, 32), nq=(32, 36), nk=(36, 40), nv=(40, 44), g_hg=(44, 52), g_cv=(52, 60),
            g_na=(60, 68))
_ORDER = ("g_hg", "g_cv", "g_na", "hi", "hg", "ca", "cb", "cc", "nq", "nk", "nv", "hq", "hzf", "hzb")
_COL = {}
for _name in _ORDER:
    _COL[_name] = sum(_SRC[m][1] - _SRC[m][0] for m in _ORDER[:_ORDER.index(_name)])
N_IN = 68 * LANES
N_STORE = _COL["hq"] * LANES

_NT = (((1,), (1,)), ((), ()))
_TN = (((0,), (0,)), ((), ()))


def _cparams(sem):
    return pltpu.CompilerParams(dimension_semantics=sem, vmem_limit_bytes=VMEM_LIMIT)


def _resident(block_shape, index_map):
    return pl.BlockSpec(block_shape, index_map, pipeline_mode=pl.Buffered(1))


def _rms_scale(x, gain):
    return x * lax.rsqrt(jnp.mean(x * x, axis=-1, keepdims=True) + RMS_EPS) * gain


def _col_chunks(n):
    return [(c, min(c + N_CHUNK, n)) for c in range(0, n, N_CHUNK)]


def _half_swiglu(x, g_ref, wg_ref, wu_ref, wd_ref):
    xn = _rms_scale(x, g_ref[...]).astype(BF16)

    def hidden(c0, c1):
        a = jnp.dot(xn, wg_ref[:, c0:c1], preferred_element_type=F32)
        b = jnp.dot(xn, wu_ref[:, c0:c1], preferred_element_type=F32)
        return (a * jax.nn.sigmoid(a) * b).astype(BF16)

    chunks = _col_chunks(wd_ref.shape[0])
    acc = None
    h = hidden(*chunks[0])
    for idx, (c0, c1) in enumerate(chunks):
        h_next = hidden(*chunks[idx + 1]) if idx + 1 < len(chunks) else None
        part = jnp.dot(h, wd_ref[c0:c1, :], preferred_element_type=F32)
        acc = part if acc is None else acc + part
        h = h_next
    return x + 0.5 * acc


def _ffn_weight_specs(d, d_ff, layer):
    return [_resident((None, 1, d), lambda i: (layer, 0, 0)),
            _resident((None, d, d_ff), lambda i: (layer, 0, 0)),
            _resident((None, d, d_ff), lambda i: (layer, 0, 1)),
            _resident((None, d_ff, d), lambda i: (layer, 0, 0))]


def _ffn_body(x_ref, g_ref, wg_ref, wu_ref, wd_ref, o_ref):
    o_ref[...] = _half_swiglu(x_ref[...], g_ref, wg_ref, wu_ref, wd_ref)


def _ffn(x, gain, w_gu, w_down, layer):
    n, d = x.shape
    assert n % TM_FFN == 0
    return pl.pallas_call(
        _ffn_body,
        grid=(n // TM_FFN,),
        in_specs=[pl.BlockSpec((TM_FFN, d), lambda i: (i, 0))]
        + _ffn_weight_specs(d, w_down.shape[1], layer),
        out_specs=pl.BlockSpec((TM_FFN, d), lambda i: (i, 0)),
        out_shape=jax.ShapeDtypeStruct((n, d), F32),
        compiler_params=_cparams(("parallel",)),
    )(x, gain, w_gu, w_gu, w_down)


def _running_sum_matrix():
    tok = np.arange(TM)
    chunk, pos = tok // HG_C, tok % HG_C
    ref_tok = np.arange(TM // HG_SB) * HG_SB + HG_SB // 2
    mats = []
    for rev in (False, True):
        end_tok = np.arange(TM // HG_C) * HG_C + (0 if rev else HG_C - 1)
        at = np.concatenate([tok, ref_tok, end_tok])
        same = chunk[None, :] == chunk[at][:, None]
        upto = pos[None, :] >= pos[at][:, None] if rev else pos[None, :] <= pos[at][:, None]
        pad = np.zeros((-len(at) % (2 * SUBLANES), TM), bool)
        mats.append(np.concatenate([same & upto, pad], axis=0))
    return jnp.asarray(np.stack(mats).astype(np.float32), BF16)


def _inproj_body(x_ref, g_ref, w_ref, lb_ref, sum_ref,
                 o_ref, qt_ref, kt_ref, kh_ref, rv_ref, be_ref):
    xn = _rms_scale(x_ref[...], g_ref[...]).astype(BF16)
    n_store = o_ref.shape[1]
    w = kt_ref.shape[1] // 2

    def proj(c0, c1):
        return jnp.dot(xn, w_ref[:, c0:c1], preferred_element_type=F32)

    q = proj(n_store, n_store + w)
    gated = []
    for side in range(2):
        z = proj(n_store + (1 + side) * w, n_store + (2 + side) * w)
        lb = lb_ref[side:side + 1, :]
        f = lb + (1.0 - lb) * jax.nn.sigmoid(z)
        gated.append((1.0 - f, jnp.log(jnp.maximum(f, F_FLOOR)).astype(BF16)))
    pieces = [(side, c0) for side in range(2) for c0 in range(0, w, MXU_N)]
    slab = _col_chunks(n_store)
    every = len(slab) // (len(pieces) + 1)
    sums = {}
    for idx, (c0, c1) in enumerate(slab):
        o_ref[:, c0:c1] = proj(c0, c1).astype(BF16)
        if (idx + 1) % every == 0 and len(sums) < len(pieces):
            side, p0 = pieces[len(sums)]
            logf = gated[side][1][:, p0:p0 + MXU_N]
            parts = [jnp.dot(sum_ref[side, r0:r0 + LANES, r0:r0 + LANES], logf[r0:r0 + LANES],
                             preferred_element_type=F32) for r0 in range(0, TM, LANES)]
            parts.append(jnp.dot(sum_ref[side, TM:, :], logf, preferred_element_type=F32))
            sums[side, p0] = jnp.concatenate(parts, axis=0)
    n_r, n_e = TM // HG_SB, TM // HG_C
    for (side, p0), full in sums.items():
        k = gated[side][0][:, p0:p0 + MXU_N]
        b = full[:TM]
        r_rows, e_rows = [], []
        for r0 in range(0, TM, HG_C):
            e = b[r0:r0 + 1, :] if side else b[r0 + HG_C - 1:r0 + HG_C, :]
            e_rows.append(jnp.broadcast_to(e, (HG_C, MXU_N)))
            for s0 in range(r0, r0 + HG_C, HG_SB):
                r = b[s0 + HG_SB // 2:s0 + HG_SB // 2 + 1, :]
                r_rows.append(jnp.broadcast_to(r, (HG_SB, MXU_N)))
        r_rows = jnp.concatenate(r_rows, axis=0)
        e_rows = jnp.concatenate(e_rows, axis=0)
        d = r_rows - b
        cols = slice(side * w + p0, side * w + p0 + MXU_N)
        kt_ref[:, cols] = (k * jnp.exp(d)).astype(BF16)
        qt_ref[:, cols] = (q[:, p0:p0 + MXU_N] * jnp.exp(-d)).astype(BF16)
        kh_ref[:, cols] = (k * jnp.exp(e_rows - b)).astype(BF16)
        rv_ref[:, cols] = full[TM:TM + n_r]
        be_ref[:, cols] = full[TM + n_r:TM + n_r + n_e]


def _inproj(x, gain, w_in, lb, sums, layer):
    n, d = x.shape
    n_in = w_in.shape[-1]
    w2 = 2 * MIX_W
    rows = lambda r: pl.BlockSpec((r, w2), lambda i: (i, 0))
    return pl.pallas_call(
        _inproj_body,
        grid=(n // TM,),
        in_specs=[
            pl.BlockSpec((TM, d), lambda i: (i, 0)),
            _resident((None, 1, d), lambda i: (layer, 0, 0)),
            _resident((None, d, n_in), lambda i: (layer, 0, 0)),
            _resident((None, 2, MIX_W), lambda i: (layer, 0, 0)),
            _resident(sums.shape, lambda i: (0, 0, 0)),
        ],
        out_specs=[pl.BlockSpec((TM, N_STORE), lambda i: (i, 0)), rows(TM), rows(TM), rows(TM),
                   rows(TM // HG_SB), rows(TM // HG_C)],
        out_shape=[jax.ShapeDtypeStruct((n, N_STORE), BF16)]
        + [jax.ShapeDtypeStruct((n, w2), BF16)] * 3
        + [jax.ShapeDtypeStruct((n // HG_SB, w2), F32),
           jax.ShapeDtypeStruct((n // HG_C, w2), F32)],
        compiler_params=_cparams(("parallel",)),
    )(x, gain, w_in, lb, sums)


def _per_block(vecs):
    return jnp.concatenate([jnp.broadcast_to(v, (HG_SB, v.shape[1])) for v in vecs], axis=0)


def _key_variants(kt, refs, rev):
    blocks = [kt[HG_SB * j:HG_SB * (j + 1), :] for j in range(HG_NSB)]
    wide = [blk.astype(F32) for blk in blocks]
    zero = jnp.zeros(blocks[0].shape, BF16)
    parts = []
    for i in range(HG_NSB):
        for j in range(HG_NSB):
            if j == i:
                parts.append(blocks[j])
            elif (j > i) if rev else (j < i):
                parts.append((wide[j] * jnp.exp(refs[i] - refs[j])).astype(BF16))
            else:
                parts.append(zero)
    return jnp.concatenate(parts, axis=0)


def _conv_gate(ca_ref, cb_ref, cc_ref, w_ref, bias_ref, o_ref):
    t_len = ca_ref.shape[0]
    z = cc_ref[...].astype(F32) * ca_ref[...].astype(F32)
    edge = lax.broadcasted_iota(jnp.int32, (SUBLANES, z.shape[1]), 0)
    prev = pltpu.roll(z, 1, axis=0)
    prev = jnp.concatenate([jnp.where(edge == 0, 0.0, prev[:SUBLANES]), prev[SUBLANES:]], axis=0)
    nxt = pltpu.roll(z, t_len - 1, axis=0)
    nxt = jnp.concatenate([nxt[:-SUBLANES],
                           jnp.where(edge == SUBLANES - 1, 0.0, nxt[-SUBLANES:])], axis=0)
    zc = prev * w_ref[0:1, :] + z * w_ref[1:2, :] + nxt * w_ref[2:3, :]
    o_ref[...] = (cb_ref[...].astype(F32) * (zc + bias_ref[...])).astype(BF16)


def _hgrn_conv_body(v_ref, g_ref, ca_ref, cb_ref, cc_ref,
                    qtf_ref, qtb_ref, ktf_ref, ktb_ref, khf_ref, khb_ref,
                    rvf_ref, rvb_ref, bef_ref, beb_ref, gn_ref, cw_ref, cbias_ref,
                    o_ref, ocv_ref, s_ref, stf_ref, stb_ref, rm_ref, oi_ref):
    t_len = v_ref.shape[0]
    nc = t_len // HG_C
    w = HG_DK
    tr = lax.broadcasted_iota(jnp.int32, (HG_C, HG_NSB * HG_C), 0)
    cr = lax.broadcasted_iota(jnp.int32, (HG_C, HG_NSB * HG_C), 1)
    own = (cr // HG_C) == (tr // HG_SB)
    src = cr % HG_C
    rmask_f = own & (src <= tr)
    rmask_b = own & (src >= tr)
    side = {False: (qtf_ref, ktf_ref, khf_ref, rvf_ref, bef_ref),
            True: (qtb_ref, ktb_ref, khb_ref, rvb_ref, beb_ref)}

    _conv_gate(ca_ref, cb_ref, cc_ref, cw_ref, cbias_ref, ocv_ref)

    def chunk_rows(c):
        return pl.ds(pl.multiple_of(c * HG_C, HG_C), HG_C)

    def park(refs, slot, items):
        for idx, vals in enumerate(items):
            for ref, val in zip(refs, vals):
                ref[slot + idx] = val

    n_blk = nc // HG_UNROLL

    def pass1(i, carry):
        first = i * HG_UNROLL
        last = nc - HG_UNROLL - first
        decs = {False: jnp.exp(bef_ref[pl.ds(pl.multiple_of(first, HG_UNROLL), HG_UNROLL), :]),
                True: jnp.exp(beb_ref[pl.ds(pl.multiple_of(last, HG_UNROLL), HG_UNROLL), :])}
        steps = []
        for u in range(HG_UNROLL):
            for rev, c, row in ((False, first + u, u), (True, nc - 1 - first - u, HG_UNROLL - 1 - u)):
                upd = lax.dot_general(v_ref[chunk_rows(c), :], side[rev][2][chunk_rows(c), :], _TN,
                                      preferred_element_type=F32)
                steps.append((rev, c, decs[rev][row:row + 1, :], upd))
        st = {False: stf_ref[...], True: stb_ref[...]}
        for rev, c, dec, upd in steps:
            s_ref[c, :, (w if rev else 0):(2 * w if rev else w)] = st[rev].astype(BF16)
            st[rev] = st[rev] * dec + upd
        stf_ref[...] = st[False]
        stb_ref[...] = st[True]
        return carry

    stf_ref[...] = jnp.zeros_like(stf_ref)
    stb_ref[...] = jnp.zeros_like(stb_ref)
    lax.fori_loop(0, n_blk, pass1, 0)

    n_blk2 = nc // HG_UNROLL2

    def scores(blk):
        pre = []
        blk_refs = pl.ds(pl.multiple_of(blk * (HG_UNROLL2 * HG_NSB), HG_UNROLL2 * HG_NSB),
                         HG_UNROLL2 * HG_NSB)
        rvs = {rev: side[rev][3][blk_refs, :] for rev in (False, True)}
        for u in range(HG_UNROLL2):
            c = blk * HG_UNROLL2 + u
            rows = chunk_rows(c)
            rs, qhs = [], []
            for rev in (False, True):
                qt_ref, kt_ref = side[rev][:2]
                refs = [rvs[rev][HG_NSB * u + j:HG_NSB * u + j + 1, :] for j in range(HG_NSB)]
                qt = qt_ref[rows, :]
                kall = _key_variants(kt_ref[rows, :], refs, rev)
                rs.append(lax.dot_general(qt, kall, _NT, preferred_element_type=F32))
                qhs.append(qt.astype(F32) * _per_block([jnp.exp(r) for r in refs]))
            pre.append((c, rs, jnp.concatenate(qhs, axis=1).astype(BF16)))
        out = []
        for c, (r_f, r_b), qh in pre:
            oi = lax.dot_general(qh, s_ref[c], _NT, preferred_element_type=F32)
            rm = (jnp.where(rmask_f, r_f, 0.0) + jnp.where(rmask_b, r_b, 0.0)).astype(BF16)
            out.append((rm, oi))
        return out

    park((rm_ref, oi_ref), 0, scores(0))

    def outputs(i, slot):
        outs = []
        for u in range(HG_UNROLL2):
            rows = chunk_rows(i * HG_UNROLL2 + u)
            v4 = jnp.concatenate([v_ref[rows, :]] * HG_NSB, axis=0)
            o = oi_ref[slot + u] + jnp.dot(rm_ref[slot + u], v4, preferred_element_type=F32)
            g = g_ref[rows, :].astype(F32)
            o = _rms_scale(o, gn_ref[...]) * (g * jax.nn.sigmoid(g))
            outs.append((rows, o.astype(BF16)))
        return outs

    def commit_outputs(outs):
        for rows, o in outs:
            o_ref[rows, :] = o

    def pass2(i, carry):
        cur = (i & 1) * HG_UNROLL2
        outs = outputs(i, cur)
        fresh = scores(i + 1)
        commit_outputs(outs)
        park((rm_ref, oi_ref), HG_UNROLL2 - cur, fresh)
        return carry

    lax.fori_loop(0, n_blk2 - 1, pass2, 0)
    commit_outputs(outputs(n_blk2 - 1, ((n_blk2 - 1) & 1) * HG_UNROLL2))


def _hgrn_conv(proj, ops, gnorm, conv_w, conv_b, layer, batch, t_len):
    assert (t_len // HG_C) % HG_UNROLL2 == 0 and HG_UNROLL2 % HG_UNROLL == 0
    assert MIX_W // CV_GROUP == HG_HEADS
    assert HG_UNROLL % SUBLANES == 0
    nc = t_len // HG_C
    qt, kt, kh, rv, be = ops
    spec = lambda c0: pl.BlockSpec((t_len, LANES), lambda b, h: (b, c0 + h))
    both = lambda rows: [pl.BlockSpec((rows, LANES), lambda b, h: (b, h)),
                         pl.BlockSpec((rows, LANES), lambda b, h: (b, HG_HEADS + h))]
    par = lambda rows: pl.BlockSpec((None, rows, LANES), lambda b, h: (layer, 0, h))
    out = jax.ShapeDtypeStruct((batch * t_len, MIX_W), BF16)
    return pl.pallas_call(
        _hgrn_conv_body,
        grid=(batch, HG_HEADS),
        in_specs=[spec(_COL["hi"]), spec(_COL["hg"]), spec(_COL["ca"]), spec(_COL["cb"]),
                  spec(_COL["cc"])]
        + both(t_len) + both(t_len) + both(t_len) + both(t_len // HG_SB) + both(nc)
        + [par(1), par(3), par(1)],
        out_specs=[pl.BlockSpec((t_len, LANES), lambda b, h: (b, h))] * 2,
        out_shape=[out, out],
        scratch_shapes=[pltpu.VMEM((nc, HG_DK, 2 * HG_DK), BF16),
                        pltpu.VMEM((HG_DK, HG_DK), F32), pltpu.VMEM((HG_DK, HG_DK), F32),
                        pltpu.VMEM((2 * HG_UNROLL2, HG_C, HG_NSB * HG_C), BF16),
                        pltpu.VMEM((2 * HG_UNROLL2, HG_C, HG_DK), F32)],
        compiler_params=_cparams(("parallel", "parallel")),
    )(proj, proj, proj, proj, proj, qt, qt, kt, kt, kh, kh, rv, rv, be, be, gnorm, conv_w, conv_b)


def _na_bias_table(rpb):
    n_layers, n_dr = rpb.shape[0], rpb.shape[2]
    qc = np.arange(GRID_W)[:, None]
    kc = np.arange(GRID_W)[None, :]
    cs = np.clip(qc - NA_KW // 2, 0, GRID_W - NA_KW)
    col_ok = (kc >= cs) & (kc < cs + NA_KW)
    dc = np.clip(kc - qc + NA_KW - 1, 0, 2 * NA_KW - 2)
    pick = (dc[None] == np.arange(2 * NA_KW - 1)[:, None, None]) & col_ok[None]
    t = jnp.einsum("lhrc,cqk->lhrqk", rpb.astype(F32), jnp.asarray(pick, F32),
                   precision=lax.Precision.HIGHEST)
    t = t + jnp.asarray(np.where(col_ok, 0.0, MASK_VALUE), F32)
    t = t.reshape(n_layers, NA_HEADS // 2, 2, n_dr, GRID_W, GRID_W).transpose(0, 1, 3, 2, 4, 5)
    t = t.reshape(n_layers, NA_HEADS // 2, n_dr, 2 * GRID_W, GRID_W)
    return jnp.concatenate([t[:, :, :-1], t[:, :, 1:]], axis=-1)


def _na_body(q_ref, k_ref, v_ref, bias_ref, o_ref):
    n_rows = q_ref.shape[0] // GRID_W
    lane = lax.broadcasted_iota(jnp.int32, (GRID_W, LANES), 1)
    first = lane < NA_DH
    m0 = jnp.where(first, NA_SCALE, 0.0)
    m1 = jnp.where(first, 0.0, NA_SCALE)
    win = NA_KH * GRID_W

    def window(r):
        rs = jnp.clip(r - NA_KH // 2, 0, n_rows - NA_KH)
        return rs, pl.ds(pl.multiple_of(rs * GRID_W, GRID_W), win)

    def logits(r):
        rs, krows = window(r)
        q = q_ref[pl.ds(pl.multiple_of(r * GRID_W, GRID_W), GRID_W), :].astype(F32)
        q2 = jnp.concatenate([q * m0, q * m1], axis=0).astype(BF16)
        first_dr = NA_KH - 1 - (r - rs)
        bias = jnp.concatenate([bias_ref[first_dr + 2 * m] for m in range(NA_KH // 2)], axis=1)
        return lax.dot_general(q2, k_ref[krows, :], _NT, preferred_element_type=F32) + bias

    def attend(s, vw):
        p = jnp.exp(s - jnp.max(s, axis=-1, keepdims=True))
        l = jnp.sum(p, axis=-1, keepdims=True)
        o2 = jnp.dot(p.astype(BF16), vw, preferred_element_type=F32) / l
        return jnp.where(first, o2[:GRID_W, :], o2[GRID_W:, :]).astype(BF16)

    def rows_step(i, carry):
        rows = [i * NA_UNROLL + u for u in range(NA_UNROLL)]
        scores = [logits(r) for r in rows]
        outs = [attend(s, v_ref[window(r)[1], :]) for s, r in zip(scores, rows)]
        for r, o in zip(rows, outs):
            o_ref[pl.ds(pl.multiple_of(r * GRID_W, GRID_W), GRID_W), :] = o
        return carry

    lax.fori_loop(0, n_rows // NA_UNROLL, rows_step, 0)


def _natten(proj, bias, layer, batch, t_len):
    assert t_len % GRID_W == 0 and t_len // GRID_W >= NA_KH
    assert (t_len // GRID_W) % NA_UNROLL == 0
    spec = lambda c0: pl.BlockSpec((t_len, LANES), lambda p, b: (b, c0 + p))
    return pl.pallas_call(
        _na_body,
        grid=(NA_HEADS // 2, batch),
        in_specs=[spec(_COL["nq"]), spec(_COL["nk"]), spec(_COL["nv"]),
                  pl.BlockSpec((None, None, 2 * NA_KH - 2, 2 * GRID_W, 2 * GRID_W),
                               lambda p, b: (layer, p, 0, 0, 0))],
        out_specs=pl.BlockSpec((t_len, LANES), lambda p, b: (b, p)),
        out_shape=jax.ShapeDtypeStruct((batch * t_len, MIX_W), BF16),
        compiler_params=_cparams(("parallel", "parallel")),
    )(proj, proj, proj, bias)


def _mix_ffn_body(x_ref, ghg_ref, gcv_ref, gna_ref, ohg_ref, ocv_ref, ona_ref,
                  whg_ref, wcv_ref, wna_ref, wout_ref, g_ref, wg_ref, wu_ref, wd_ref, *rest, final):
    o_ref = rest[-1]
    ys = [jnp.dot(a_ref[...], w_ref[...], preferred_element_type=F32)
          for a_ref, w_ref in ((ohg_ref, whg_ref), (ocv_ref, wcv_ref), (ona_ref, wna_ref))]
    m = sum(jax.nn.sigmoid(g_ref[...].astype(F32)) * y
            for g_ref, y in zip((ghg_ref, gcv_ref, gna_ref), ys))
    x = x_ref[...] + jnp.dot(m.astype(BF16), wout_ref[...], preferred_element_type=F32)
    y = _half_swiglu(x, g_ref, wg_ref, wu_ref, wd_ref)
    if final:
        y = _rms_scale(y, rest[0][...])
    o_ref[...] = y


def _mix_ffn(x, proj, o_hg, o_cv, o_na, w_hg, w_cv, w_na, w_out, gain, w_gu, w_down, layer,
             final_gain=None):
    n, d = x.shape
    final = final_gain is not None
    gate = lambda c0: pl.BlockSpec((TM, d), lambda i: (i, c0 * LANES // d))
    act = pl.BlockSpec((TM, MIX_W), lambda i: (i, 0))
    wmix = _resident((None, MIX_W, d), lambda i: (layer, 0, 0))
    in_specs = [pl.BlockSpec((TM, d), lambda i: (i, 0)),
                gate(_COL["g_hg"]), gate(_COL["g_cv"]), gate(_COL["g_na"]),
                act, act, act, wmix, wmix, wmix,
                _resident((None, d, d), lambda i: (layer, 0, 0))]
    in_specs += _ffn_weight_specs(d, w_down.shape[1], layer)
    args = [x, proj, proj, proj, o_hg, o_cv, o_na, w_hg, w_cv, w_na, w_out,
            gain, w_gu, w_gu, w_down]
    if final:
        in_specs.append(_resident((1, d), lambda i: (0, 0)))
        args.append(final_gain)
    return pl.pallas_call(
        functools.partial(_mix_ffn_body, final=final),
        grid=(n // TM,),
        in_specs=in_specs,
        out_specs=pl.BlockSpec((TM, d), lambda i: (i, 0)),
        out_shape=jax.ShapeDtypeStruct((n, d), F32),
        compiler_params=_cparams(("parallel",)),
    )(*args)


def _lower_bounds(lb_logits):
    p = jax.nn.softmax(lb_logits.astype(F32), axis=1)
    return jnp.cumsum(p, axis=1) - p[:, :1]


def _trunk(x3, p):
    batch, t_len, d = x3.shape
    assert d % LANES == 0 and t_len % HG_C == 0 and (batch * t_len) % TM == 0
    n_layers = p["n_layers"]
    x = x3.reshape(batch * t_len, d)
    for l in range(n_layers):
        x = _ffn(x, p["ffn1_norm"], p["ffn1_w_gu"], p["ffn1_w_down"], l)
        proj, *hg_ops = _inproj(x, p["mix_norm"], p["w_in"], p["lb"], p["run_sums"], l)
        o_hg, o_cv = _hgrn_conv(proj, hg_ops, p["hg_out_norm"], p["conv_w"], p["conv_b"],
                                l, batch, t_len)
        o_na = _natten(proj, p["na_bias"], l, batch, t_len)
        x = _mix_ffn(x, proj, o_hg, o_cv, o_na, p["w_hg_out"], p["w_cv_out"], p["w_na_out"],
                     p["w_out"], p["ffn2_norm"], p["ffn2_w_gu"], p["ffn2_w_down"], l,
                     final_gain=p["final_norm"] if l == n_layers - 1 else None)
    return x.reshape(batch, t_len, d)


def kernel(x_prompt, x_sample, ffn1_norm, ffn1_w_gu, ffn1_w_down, mix_norm, w_in, hg_lb_logits, hg_out_norm, w_hg_out, conv_w, conv_b, w_cv_out, na_rpb, w_na_out, w_out, ffn2_norm, ffn2_w_gu, ffn2_w_down, final_norm):
    n_layers, d, n_in = w_in.shape
    assert n_in == N_IN
    row = lambda a: a.astype(F32)[:, None, :]
    p = dict(
        n_layers=n_layers,
        ffn1_norm=row(ffn1_norm), ffn1_w_gu=ffn1_w_gu.astype(BF16), ffn1_w_down=ffn1_w_down.astype(BF16),
        ffn2_norm=row(ffn2_norm), ffn2_w_gu=ffn2_w_gu.astype(BF16), ffn2_w_down=ffn2_w_down.astype(BF16),
        mix_norm=row(mix_norm),
        w_in=jnp.concatenate([w_in[..., _SRC[m][0] * LANES:_SRC[m][1] * LANES] for m in _ORDER],
                             axis=-1).astype(BF16),
        lb=_lower_bounds(hg_lb_logits).transpose(1, 0, 2),
        run_sums=_running_sum_matrix(),
        hg_out_norm=row(hg_out_norm),
        w_hg_out=w_hg_out.astype(BF16), w_cv_out=w_cv_out.astype(BF16), w_na_out=w_na_out.astype(BF16),
        conv_w=conv_w.astype(F32), conv_b=row(conv_b),
        na_bias=_na_bias_table(na_rpb),
        w_out=w_out.astype(BF16),
        final_norm=final_norm.astype(F32)[None, :],
    )
    return _trunk(x_prompt, p), _trunk(x_sample, p)
```

```python
import functools

import numpy as np
import jax
import jax.numpy as jnp
from jax import lax
from jax.experimental import pallas as pl
from jax.experimental.pallas import tpu as pltpu

F32 = jnp.float32
BF16 = jnp.bfloat16

GRID_W = 64
RMS_EPS = 1e-6
F_FLOOR = 1e-30
MASK_VALUE = -1e30
HG_HEADS, HG_DK = 4, 128
CV_GROUP = 128
NA_HEADS, NA_DH = 8, 64
NA_KH, NA_KW = 8, 16
NA_SCALE = NA_DH ** -0.5
MIX_W = HG_HEADS * HG_DK

LANES = 128
SUBLANES = 8
MXU_N = 256
VMEM_LIMIT = 56 * 1024 * 1024

TM = 512
TM_FFN = 1024
N_CHUNK = 4 * MXU_N
HG_C = 64
HG_SB = 16
HG_NSB = HG_C // HG_SB
HG_UNROLL = 16
HG_UNROLL2 = 32
NA_UNROLL = 32

_SRC = dict(hq=(0, 4), hi=(4, 8), hzf=(8, 12), hzb=(12, 16), hg=(16, 20), ca=(20, 24), cb=(24, 28),
            cc=(28, 32), nq=(32, 36), nk=(36, 40), nv=(40, 44), g_hg=(44, 52), g_cv=(52, 60),
            g_na=(60, 68))
_ORDER = ("g_hg", "g_cv", "g_na", "hi", "hg", "ca", "cb", "cc", "nq", "nk", "nv", "hq", "hzf", "hzb")
_COL = {}
for _name in _ORDER:
    _COL[_name] = sum(_SRC[m][1] - _SRC[m][0] for m in _ORDER[:_ORDER.index(_name)])
N_IN = 68 * LANES
N_STORE = _COL["hq"] * LANES

_NT = (((1,), (1,)), ((), ()))
_TN = (((0,), (0,)), ((), ()))


def _cparams(sem):
    return pltpu.CompilerParams(dimension_semantics=sem, vmem_limit_bytes=VMEM_LIMIT)


def _resident(block_shape, index_map):
    return pl.BlockSpec(block_shape, index_map, pipeline_mode=pl.Buffered(1))


def _rms_scale(x, gain):
    return x * lax.rsqrt(jnp.mean(x * x, axis=-1, keepdims=True) + RMS_EPS) * gain


def _col_chunks(n):
    return [(c, min(c + N_CHUNK, n)) for c in range(0, n, N_CHUNK)]


def _half_swiglu(x, g_ref, wg_ref, wu_ref, wd_ref):
    xn = _rms_scale(x, g_ref[...]).astype(BF16)

    def hidden(c0, c1):
        a = jnp.dot(xn, wg_ref[:, c0:c1], preferred_element_type=F32)
        b = jnp.dot(xn, wu_ref[:, c0:c1], preferred_element_type=F32)
        return (a * jax.nn.sigmoid(a) * b).astype(BF16)

    chunks = _col_chunks(wd_ref.shape[0])
    acc = None
    h = hidden(*chunks[0])
    for idx, (c0, c1) in enumerate(chunks):
        h_next = hidden(*chunks[idx + 1]) if idx + 1 < len(chunks) else None
        part = jnp.dot(h, wd_ref[c0:c1, :], preferred_element_type=F32)
        acc = part if acc is None else acc + part
        h = h_next
    return x + 0.5 * acc


def _ffn_weight_specs(d, d_ff, layer):
    return [_resident((None, 1, d), lambda i: (layer, 0, 0)),
            _resident((None, d, d_ff), lambda i: (layer, 0, 0)),
            _resident((None, d, d_ff), lambda i: (layer, 0, 1)),
            _resident((None, d_ff, d), lambda i: (layer, 0, 0))]


def _ffn_body(x_ref, g_ref, wg_ref, wu_ref, wd_ref, o_ref):
    o_ref[...] = _half_swiglu(x_ref[...], g_ref, wg_ref, wu_ref, wd_ref)


def _ffn(x, gain, w_gu, w_down, layer):
    n, d = x.shape
    assert n % TM_FFN == 0
    return pl.pallas_call(
        _ffn_body,
        grid=(n // TM_FFN,),
        in_specs=[pl.BlockSpec((TM_FFN, d), lambda i: (i, 0))]
        + _ffn_weight_specs(d, w_down.shape[1], layer),
        out_specs=pl.BlockSpec((TM_FFN, d), lambda i: (i, 0)),
        out_shape=jax.ShapeDtypeStruct((n, d), F32),
        compiler_params=pltpu.CompilerParams(
            dimension_semantics=("parallel",), vmem_limit_bytes=VMEM_LIMIT,
            allow_input_fusion=[False, False, True, True, True]),
    )(x, gain, w_gu, w_gu, w_down)


def _running_sum_matrix():
    tok = np.arange(TM)
    chunk, pos = tok // HG_C, tok % HG_C
    ref_tok = np.arange(TM // HG_SB) * HG_SB + HG_SB // 2
    mats = []
    for rev in (False, True):
        end_tok = np.arange(TM // HG_C) * HG_C + (0 if rev else HG_C - 1)
        at = np.concatenate([tok, ref_tok, end_tok])
        same = chunk[None, :] == chunk[at][:, None]
        upto = pos[None, :] >= pos[at][:, None] if rev else pos[None, :] <= pos[at][:, None]
        pad = np.zeros((-len(at) % (2 * SUBLANES), TM), bool)
        mats.append(np.concatenate([same & upto, pad], axis=0))
    return jnp.asarray(np.stack(mats).astype(np.float32), BF16)


def _inproj_body(x_ref, g_ref, w_ref, lb_ref, sum_ref,
                 o_ref, qt_ref, kt_ref, kh_ref, rv_ref, be_ref):
    xn = _rms_scale(x_ref[...], g_ref[...]).astype(BF16)
    n_store = o_ref.shape[1]
    w = kt_ref.shape[1] // 2

    def proj(c0, c1):
        return jnp.dot(xn, w_ref[:, c0:c1], preferred_element_type=F32)

    q = proj(n_store, n_store + w)
    gated = []
    for side in range(2):
        z = proj(n_store + (1 + side) * w, n_store + (2 + side) * w)
        lb = lb_ref[side:side + 1, :]
        f = lb + (1.0 - lb) * jax.nn.sigmoid(z)
        gated.append((1.0 - f, jnp.log(jnp.maximum(f, F_FLOOR)).astype(BF16)))
    pieces = [(side, c0) for side in range(2) for c0 in range(0, w, MXU_N)]
    slab = _col_chunks(n_store)
    every = len(slab) // (len(pieces) + 1)
    sums = {}
    for idx, (c0, c1) in enumerate(slab):
        o_ref[:, c0:c1] = proj(c0, c1).astype(BF16)
        if (idx + 1) % every == 0 and len(sums) < len(pieces):
            side, p0 = pieces[len(sums)]
            logf = gated[side][1][:, p0:p0 + MXU_N]
            parts = [jnp.dot(sum_ref[side, r0:r0 + LANES, r0:r0 + LANES], logf[r0:r0 + LANES],
                             preferred_element_type=F32) for r0 in range(0, TM, LANES)]
            parts.append(jnp.dot(sum_ref[side, TM:, :], logf, preferred_element_type=F32))
            sums[side, p0] = jnp.concatenate(parts, axis=0)
    n_r, n_e = TM // HG_SB, TM // HG_C
    for (side, p0), full in sums.items():
        k = gated[side][0][:, p0:p0 + MXU_N]
        b = full[:TM]
        r_rows, e_rows = [], []
        for r0 in range(0, TM, HG_C):
            e = b[r0:r0 + 1, :] if side else b[r0 + HG_C - 1:r0 + HG_C, :]
            e_rows.append(jnp.broadcast_to(e, (HG_C, MXU_N)))
            for s0 in range(r0, r0 + HG_C, HG_SB):
                r = b[s0 + HG_SB // 2:s0 + HG_SB // 2 + 1, :]
                r_rows.append(jnp.broadcast_to(r, (HG_SB, MXU_N)))
        r_rows = jnp.concatenate(r_rows, axis=0)
        e_rows = jnp.concatenate(e_rows, axis=0)
        d = r_rows - b
        cols = slice(side * w + p0, side * w + p0 + MXU_N)
        kt_ref[:, cols] = (k * jnp.exp(d)).astype(BF16)
        qt_ref[:, cols] = (q[:, p0:p0 + MXU_N] * jnp.exp(-d)).astype(BF16)
        kh_ref[:, cols] = (k * jnp.exp(e_rows - b)).astype(BF16)
        rv_ref[:, cols] = full[TM:TM + n_r]
        be_ref[:, cols] = full[TM + n_r:TM + n_r + n_e]


def _inproj(x, gain, w_in, lb, sums, layer):
    n, d = x.shape
    n_in = w_in.shape[-1]
    w2 = 2 * MIX_W
    rows = lambda r: pl.BlockSpec((r, w2), lambda i: (i, 0))
    return pl.pallas_call(
        _inproj_body,
        grid=(n // TM,),
        in_specs=[
            pl.BlockSpec((TM, d), lambda i: (i, 0)),
            _resident((None, 1, d), lambda i: (layer, 0, 0)),
            _resident((None, d, n_in), lambda i: (layer, 0, 0)),
            _resident((None, 2, MIX_W), lambda i: (layer, 0, 0)),
            _resident(sums.shape, lambda i: (0, 0, 0)),
        ],
        out_specs=[pl.BlockSpec((TM, N_STORE), lambda i: (i, 0)), rows(TM), rows(TM), rows(TM),
                   rows(TM // HG_SB), rows(TM // HG_C)],
        out_shape=[jax.ShapeDtypeStruct((n, N_STORE), BF16)]
        + [jax.ShapeDtypeStruct((n, w2), BF16)] * 3
        + [jax.ShapeDtypeStruct((n // HG_SB, w2), F32),
           jax.ShapeDtypeStruct((n // HG_C, w2), F32)],
        compiler_params=_cparams(("parallel",)),
    )(x, gain, w_in, lb, sums)


def _per_block(vecs):
    return jnp.concatenate([jnp.broadcast_to(v, (HG_SB, v.shape[1])) for v in vecs], axis=0)


def _key_variants(kt, refs, rev):
    blocks = [kt[HG_SB * j:HG_SB * (j + 1), :] for j in range(HG_NSB)]
    wide = [blk.astype(F32) for blk in blocks]
    zero = jnp.zeros(blocks[0].shape, BF16)
    parts = []
    for i in range(HG_NSB):
        for j in range(HG_NSB):
            if j == i:
                parts.append(blocks[j])
            elif (j > i) if rev else (j < i):
                parts.append((wide[j] * jnp.exp(refs[i] - refs[j])).astype(BF16))
            else:
                parts.append(zero)
    return jnp.concatenate(parts, axis=0)


def _conv_gate(ca_ref, cb_ref, cc_ref, w_ref, bias_ref, o_ref):
    t_len = ca_ref.shape[0]
    z = cc_ref[...].astype(F32) * ca_ref[...].astype(F32)
    edge = lax.broadcasted_iota(jnp.int32, (SUBLANES, z.shape[1]), 0)
    prev = pltpu.roll(z, 1, axis=0)
    prev = jnp.concatenate([jnp.where(edge == 0, 0.0, prev[:SUBLANES]), prev[SUBLANES:]], axis=0)
    nxt = pltpu.roll(z, t_len - 1, axis=0)
    nxt = jnp.concatenate([nxt[:-SUBLANES],
                           jnp.where(edge == SUBLANES - 1, 0.0, nxt[-SUBLANES:])], axis=0)
    zc = prev * w_ref[0:1, :] + z * w_ref[1:2, :] + nxt * w_ref[2:3, :]
    o_ref[...] = (cb_ref[...].astype(F32) * (zc + bias_ref[...])).astype(BF16)


def _hgrn_conv_body(v_ref, g_ref, ca_ref, cb_ref, cc_ref,
                    qtf_ref, qtb_ref, ktf_ref, ktb_ref, khf_ref, khb_ref,
                    rvf_ref, rvb_ref, bef_ref, beb_ref, gn_ref, cw_ref, cbias_ref,
                    o_ref, ocv_ref, s_ref, stf_ref, stb_ref, rm_ref, oi_ref):
    t_len = v_ref.shape[0]
    nc = t_len // HG_C
    w = HG_DK
    tr = lax.broadcasted_iota(jnp.int32, (HG_C, HG_NSB * HG_C), 0)
    cr = lax.broadcasted_iota(jnp.int32, (HG_C, HG_NSB * HG_C), 1)
    own = (cr // HG_C) == (tr // HG_SB)
    src = cr % HG_C
    rmask_f = own & (src <= tr)
    rmask_b = own & (src >= tr)
    side = {False: (qtf_ref, ktf_ref, khf_ref, rvf_ref, bef_ref),
            True: (qtb_ref, ktb_ref, khb_ref, rvb_ref, beb_ref)}

    _conv_gate(ca_ref, cb_ref, cc_ref, cw_ref, cbias_ref, ocv_ref)

    def chunk_rows(c):
        return pl.ds(pl.multiple_of(c * HG_C, HG_C), HG_C)

    def park(refs, slot, items):
        for idx, vals in enumerate(items):
            for ref, val in zip(refs, vals):
                ref[slot + idx] = val

    n_blk = nc // HG_UNROLL

    def pass1(i, carry):
        first = i * HG_UNROLL
        last = nc - HG_UNROLL - first
        decs = {False: jnp.exp(bef_ref[pl.ds(pl.multiple_of(first, HG_UNROLL), HG_UNROLL), :]),
                True: jnp.exp(beb_ref[pl.ds(pl.multiple_of(last, HG_UNROLL), HG_UNROLL), :])}
        steps = []
        for u in range(HG_UNROLL):
            for rev, c, row in ((False, first + u, u), (True, nc - 1 - first - u, HG_UNROLL - 1 - u)):
                upd = lax.dot_general(v_ref[chunk_rows(c), :], side[rev][2][chunk_rows(c), :], _TN,
                                      preferred_element_type=F32)
                steps.append((rev, c, decs[rev][row:row + 1, :], upd))
        st = {False: stf_ref[...], True: stb_ref[...]}
        for rev, c, dec, upd in steps:
            s_ref[c, :, (w if rev else 0):(2 * w if rev else w)] = st[rev].astype(BF16)
            st[rev] = st[rev] * dec + upd
        stf_ref[...] = st[False]
        stb_ref[...] = st[True]
        return carry

    stf_ref[...] = jnp.zeros_like(stf_ref)
    stb_ref[...] = jnp.zeros_like(stb_ref)
    lax.fori_loop(0, n_blk, pass1, 0)

    n_blk2 = nc // HG_UNROLL2

    def scores(blk):
        pre = []
        blk_refs = pl.ds(pl.multiple_of(blk * (HG_UNROLL2 * HG_NSB), HG_UNROLL2 * HG_NSB),
                         HG_UNROLL2 * HG_NSB)
        rvs = {rev: side[rev][3][blk_refs, :] for rev in (False, True)}
        for u in range(HG_UNROLL2):
            c = blk * HG_UNROLL2 + u
            rows = chunk_rows(c)
            rs, qhs = [], []
            for rev in (False, True):
                qt_ref, kt_ref = side[rev][:2]
                refs = [rvs[rev][HG_NSB * u + j:HG_NSB * u + j + 1, :] for j in range(HG_NSB)]
                qt = qt_ref[rows, :]
                kall = _key_variants(kt_ref[rows, :], refs, rev)
                rs.append(lax.dot_general(qt, kall, _NT, preferred_element_type=F32))
                qhs.append(qt.astype(F32) * _per_block([jnp.exp(r) for r in refs]))
            pre.append((c, rs, jnp.concatenate(qhs, axis=1).astype(BF16)))
        out = []
        for c, (r_f, r_b), qh in pre:
            oi = lax.dot_general(qh, s_ref[c], _NT, preferred_element_type=F32)
            rm = (jnp.where(rmask_f, r_f, 0.0) + jnp.where(rmask_b, r_b, 0.0)).astype(BF16)
            out.append((rm, oi))
        return out

    park((rm_ref, oi_ref), 0, scores(0))

    def outputs(i, slot):
        outs = []
        for u in range(HG_UNROLL2):
            rows = chunk_rows(i * HG_UNROLL2 + u)
            v4 = jnp.concatenate([v_ref[rows, :]] * HG_NSB, axis=0)
            o = oi_ref[slot + u] + jnp.dot(rm_ref[slot + u], v4, preferred_element_type=F32)
            g = g_ref[rows, :].astype(F32)
            o = _rms_scale(o, gn_ref[...]) * (g * jax.nn.sigmoid(g))
            outs.append((rows, o.astype(BF16)))
        return outs

    def commit_outputs(outs):
        for rows, o in outs:
            o_ref[rows, :] = o

    def pass2(i, carry):
        cur = (i & 1) * HG_UNROLL2
        outs = outputs(i, cur)
        fresh = scores(i + 1)
        commit_outputs(outs)
        park((rm_ref, oi_ref), HG_UNROLL2 - cur, fresh)
        return carry

    lax.fori_loop(0, n_blk2 - 1, pass2, 0)
    commit_outputs(outputs(n_blk2 - 1, ((n_blk2 - 1) & 1) * HG_UNROLL2))


def _hgrn_conv(proj, ops, gnorm, conv_w, conv_b, layer, batch, t_len):
    assert (t_len // HG_C) % HG_UNROLL2 == 0 and HG_UNROLL2 % HG_UNROLL == 0
    assert MIX_W // CV_GROUP == HG_HEADS
    assert HG_UNROLL % SUBLANES == 0
    nc = t_len // HG_C
    qt, kt, kh, rv, be = ops
    spec = lambda c0: pl.BlockSpec((t_len, LANES), lambda b, h: (b, c0 + h))
    both = lambda rows: [pl.BlockSpec((rows, LANES), lambda b, h: (b, h)),
                         pl.BlockSpec((rows, LANES), lambda b, h: (b, HG_HEADS + h))]
    par = lambda rows: pl.BlockSpec((None, rows, LANES), lambda b, h: (layer, 0, h))
    out = jax.ShapeDtypeStruct((batch * t_len, MIX_W), BF16)
    return pl.pallas_call(
        _hgrn_conv_body,
        grid=(batch, HG_HEADS),
        in_specs=[spec(_COL["hi"]), spec(_COL["hg"]), spec(_COL["ca"]), spec(_COL["cb"]),
                  spec(_COL["cc"])]
        + both(t_len) + both(t_len) + both(t_len) + both(t_len // HG_SB) + both(nc)
        + [par(1), par(3), par(1)],
        out_specs=[pl.BlockSpec((t_len, LANES), lambda b, h: (b, h))] * 2,
        out_shape=[out, out],
        scratch_shapes=[pltpu.VMEM((nc, HG_DK, 2 * HG_DK), BF16),
                        pltpu.VMEM((HG_DK, HG_DK), F32), pltpu.VMEM((HG_DK, HG_DK), F32),
                        pltpu.VMEM((2 * HG_UNROLL2, HG_C, HG_NSB * HG_C), BF16),
                        pltpu.VMEM((2 * HG_UNROLL2, HG_C, HG_DK), F32)],
        compiler_params=_cparams(("parallel", "parallel")),
    )(proj, proj, proj, proj, proj, qt, qt, kt, kt, kh, kh, rv, rv, be, be, gnorm, conv_w, conv_b)


def _na_bias_table(rpb):
    n_layers, n_dr = rpb.shape[0], rpb.shape[2]
    qc = np.arange(GRID_W)[:, None]
    kc = np.arange(GRID_W)[None, :]
    cs = np.clip(qc - NA_KW // 2, 0, GRID_W - NA_KW)
    col_ok = (kc >= cs) & (kc < cs + NA_KW)
    dc = np.clip(kc - qc + NA_KW - 1, 0, 2 * NA_KW - 2)
    pick = (dc[None] == np.arange(2 * NA_KW - 1)[:, None, None]) & col_ok[None]
    t = jnp.einsum("lhrc,cqk->lhrqk", rpb.astype(F32), jnp.asarray(pick, F32),
                   precision=lax.Precision.HIGHEST)
    t = t + jnp.asarray(np.where(col_ok, 0.0, MASK_VALUE), F32)
    t = t.reshape(n_layers, NA_HEADS // 2, 2, n_dr, GRID_W, GRID_W).transpose(0, 1, 3, 2, 4, 5)
    t = t.reshape(n_layers, NA_HEADS // 2, n_dr, 2 * GRID_W, GRID_W)
    return jnp.concatenate([t[:, :, :-1], t[:, :, 1:]], axis=-1)


def _na_body(q_ref, k_ref, v_ref, bias_ref, o_ref):
    n_rows = q_ref.shape[0] // GRID_W
    lane = lax.broadcasted_iota(jnp.int32, (GRID_W, LANES), 1)
    first = lane < NA_DH
    m0 = jnp.where(first, NA_SCALE, 0.0)
    m1 = jnp.where(first, 0.0, NA_SCALE)
    win = NA_KH * GRID_W

    def window(r):
        rs = jnp.clip(r - NA_KH // 2, 0, n_rows - NA_KH)
        return rs, pl.ds(pl.multiple_of(rs * GRID_W, GRID_W), win)

    def logits(r):
        rs, krows = window(r)
        q = q_ref[pl.ds(pl.multiple_of(r * GRID_W, GRID_W), GRID_W), :].astype(F32)
        q2 = jnp.concatenate([q * m0, q * m1], axis=0).astype(BF16)
        first_dr = NA_KH - 1 - (r - rs)
        bias = jnp.concatenate([bias_ref[first_dr + 2 * m] for m in range(NA_KH // 2)], axis=1)
        return lax.dot_general(q2, k_ref[krows, :], _NT, preferred_element_type=F32) + bias

    def attend(s, vw):
        p = jnp.exp(s - jnp.max(s, axis=-1, keepdims=True))
        l = jnp.sum(p, axis=-1, keepdims=True)
        o2 = jnp.dot(p.astype(BF16), vw, preferred_element_type=F32) / l
        return jnp.where(first, o2[:GRID_W, :], o2[GRID_W:, :]).astype(BF16)

    def rows_step(i, carry):
        rows = [i * NA_UNROLL + u for u in range(NA_UNROLL)]
        scores = [logits(r) for r in rows]
        outs = [attend(s, v_ref[window(r)[1], :]) for s, r in zip(scores, rows)]
        for r, o in zip(rows, outs):
            o_ref[pl.ds(pl.multiple_of(r * GRID_W, GRID_W), GRID_W), :] = o
        return carry

    lax.fori_loop(0, n_rows // NA_UNROLL, rows_step, 0)


def _natten(proj, bias, layer, batch, t_len):
    assert t_len % GRID_W == 0 and t_len // GRID_W >= NA_KH
    assert (t_len // GRID_W) % NA_UNROLL == 0
    spec = lambda c0: pl.BlockSpec((t_len, LANES), lambda p, b: (b, c0 + p))
    return pl.pallas_call(
        _na_body,
        grid=(NA_HEADS // 2, batch),
        in_specs=[spec(_COL["nq"]), spec(_COL["nk"]), spec(_COL["nv"]),
                  pl.BlockSpec((None, None, 2 * NA_KH - 2, 2 * GRID_W, 2 * GRID_W),
                               lambda p, b: (layer, p, 0, 0, 0))],
        out_specs=pl.BlockSpec((t_len, LANES), lambda p, b: (b, p)),
        out_shape=jax.ShapeDtypeStruct((batch * t_len, MIX_W), BF16),
        compiler_params=_cparams(("parallel", "parallel")),
    )(proj, proj, proj, bias)


def _mix_ffn_body(x_ref, ghg_ref, gcv_ref, gna_ref, ohg_ref, ocv_ref, ona_ref,
                  whg_ref, wcv_ref, wna_ref, wout_ref, g_ref, wg_ref, wu_ref, wd_ref, *rest, final):
    o_ref = rest[-1]
    ys = [jnp.dot(a_ref[...], w_ref[...], preferred_element_type=F32)
          for a_ref, w_ref in ((ohg_ref, whg_ref), (ocv_ref, wcv_ref), (ona_ref, wna_ref))]
    m = sum(jax.nn.sigmoid(g_ref[...].astype(F32)) * y
            for g_ref, y in zip((ghg_ref, gcv_ref, gna_ref), ys))
    x = x_ref[...] + jnp.dot(m.astype(BF16), wout_ref[...], preferred_element_type=F32)
    y = _half_swiglu(x, g_ref, wg_ref, wu_ref, wd_ref)
    if final:
        y = _rms_scale(y, rest[0][...])
    o_ref[...] = y


def _mix_ffn(x, proj, o_hg, o_cv, o_na, w_hg, w_cv, w_na, w_out, gain, w_gu, w_down, layer,
             final_gain=None):
    n, d = x.shape
    final = final_gain is not None
    gate = lambda c0: pl.BlockSpec((TM, d), lambda i: (i, c0 * LANES // d))
    act = pl.BlockSpec((TM, MIX_W), lambda i: (i, 0))
    wmix = _resident((None, MIX_W, d), lambda i: (layer, 0, 0))
    in_specs = [pl.BlockSpec((TM, d), lambda i: (i, 0)),
                gate(_COL["g_hg"]), gate(_COL["g_cv"]), gate(_COL["g_na"]),
                act, act, act, wmix, wmix, wmix,
                _resident((None, d, d), lambda i: (layer, 0, 0))]
    in_specs += _ffn_weight_specs(d, w_down.shape[1], layer)
    args = [x, proj, proj, proj, o_hg, o_cv, o_na, w_hg, w_cv, w_na, w_out,
            gain, w_gu, w_gu, w_down]
    if final:
        in_specs.append(_resident((1, d), lambda i: (0, 0)))
        args.append(final_gain)
    return pl.pallas_call(
        functools.partial(_mix_ffn_body, final=final),
        grid=(n // TM,),
        in_specs=in_specs,
        out_specs=pl.BlockSpec((TM, d), lambda i: (i, 0)),
        out_shape=jax.ShapeDtypeStruct((n, d), F32),
        compiler_params=_cparams(("parallel",)),
    )(*args)


def _lower_bounds(lb_logits):
    p = jax.nn.softmax(lb_logits.astype(F32), axis=1)
    return jnp.cumsum(p, axis=1) - p[:, :1]


def _trunk(x3, p):
    batch, t_len, d = x3.shape
    assert d % LANES == 0 and t_len % HG_C == 0 and (batch * t_len) % TM == 0
    n_layers = p["n_layers"]
    x = x3.reshape(batch * t_len, d)
    for l in range(n_layers):
        x = _ffn(x, p["ffn1_norm"], p["ffn1_w_gu"], p["ffn1_w_down"], l)
        proj, *hg_ops = _inproj(x, p["mix_norm"], p["w_in"], p["lb"], p["run_sums"], l)
        o_hg, o_cv = _hgrn_conv(proj, hg_ops, p["hg_out_norm"], p["conv_w"], p["conv_b"],
                                l, batch, t_len)
        o_na = _natten(proj, p["na_bias"], l, batch, t_len)
        x = _mix_ffn(x, proj, o_hg, o_cv, o_na, p["w_hg_out"], p["w_cv_out"], p["w_na_out"],
                     p["w_out"], p["ffn2_norm"], p["ffn2_w_gu"], p["ffn2_w_down"], l,
                     final_gain=p["final_norm"] if l == n_layers - 1 else None)
    return x.reshape(batch, t_len, d)


def kernel(x_prompt, x_sample, ffn1_norm, ffn1_w_gu, ffn1_w_down, mix_norm, w_in, hg_lb_logits, hg_out_norm, w_hg_out, conv_w, conv_b, w_cv_out, na_rpb, w_na_out, w_out, ffn2_norm, ffn2_w_gu, ffn2_w_down, final_norm):
    n_layers, d, n_in = w_in.shape
    assert n_in == N_IN
    row = lambda a: a.astype(F32)[:, None, :]
    p = dict(
        n_layers=n_layers,
        ffn1_norm=row(ffn1_norm), ffn1_w_gu=ffn1_w_gu.astype(BF16), ffn1_w_down=ffn1_w_down.astype(BF16),
        ffn2_norm=row(ffn2_norm), ffn2_w_gu=ffn2_w_gu.astype(BF16), ffn2_w_down=ffn2_w_down.astype(BF16),
        mix_norm=row(mix_norm),
        w_in=jnp.concatenate([w_in[..., _SRC[m][0] * LANES:_SRC[m][1] * LANES] for m in _ORDER],
                             axis=-1).astype(BF16),
        lb=_lower_bounds(hg_lb_logits).transpose(1, 0, 2),
        run_sums=_running_sum_matrix(),
        hg_out_norm=row(hg_out_norm),
        w_hg_out=w_hg_out.astype(BF16), w_cv_out=w_cv_out.astype(BF16), w_na_out=w_na_out.astype(BF16),
        conv_w=conv_w.astype(F32), conv_b=row(conv_b),
        na_bias=_na_bias_table(na_rpb),
        w_out=w_out.astype(BF16),
        final_norm=final_norm.astype(F32)[None, :],
    )
    return _trunk(x_prompt, p), _trunk(x_sample, p)
```
